```python
import jax, jax.numpy as jnp
from jax import lax
import numpy as np

D_MODEL = 1024
BATCH = 8
SEQ = 4096
DEPTH = 4

CHUNK = 64
POOL_WINDOWS = (2, 4, 8, 16)
POOL_GROUPS = len(POOL_WINDOWS)
D_POOL = D_MODEL // 2
POOL_GC = D_POOL // POOL_GROUPS
RWKV_HEAD = 64
D_RWKV = D_MODEL
RWKV_HEADS = D_RWKV // RWKV_HEAD
D_DECAY_LORA = 64
D_AAA_LORA = 64
D_GATE_LORA = 160
D_FF = 4 * D_MODEL
N_BRANCH = 2
D_SHIFTED = 3 * D_RWKV + D_DECAY_LORA + D_AAA_LORA + D_GATE_LORA
D_IN = D_POOL + D_SHIFTED + N_BRANCH * D_MODEL
SHIFT_SPLITS = (D_RWKV, 2 * D_RWKV, 3 * D_RWKV, 3 * D_RWKV + D_DECAY_LORA,
                3 * D_RWKV + D_DECAY_LORA + D_AAA_LORA)
ALPHA = (2 * DEPTH) ** 0.25
BETA = (8 * DEPTH) ** -0.25
LN_EPS = 1e-5
GN_EPS = 64e-5

kernel_name = "hybrid_pool_rwkv7_adaln_deepnorm"


def layer_norm(x, g, b, eps):
    xf = x.astype(jnp.float32)
    mu = jnp.mean(xf, axis=-1, keepdims=True)
    var = jnp.mean(jnp.square(xf - mu), axis=-1, keepdims=True)
    return (xf - mu) * lax.rsqrt(var + eps) * g.astype(jnp.float32) + b.astype(jnp.float32)


def token_shift(p):
    return jnp.pad(p, ((0, 0), (1, 0), (0, 0)))[:, :-1]


def multiscale_pool(u):
    b, s, _ = u.shape
    ug = u.astype(jnp.float32).reshape(b, s, POOL_GROUPS, POOL_GC)
    cs = jnp.cumsum(ug, axis=1)
    count = jnp.arange(1, s + 1, dtype=jnp.float32)
    outs = []
    for gi, w in enumerate(POOL_WINDOWS):
        cg = cs[:, :, gi]
        lag = jnp.pad(cg[:, : s - w], ((0, 0), (w, 0), (0, 0)))
        cnt = jnp.minimum(count, float(w))
        outs.append((cg - lag) / cnt[None, :, None] - ug[:, :, gi])
    return jnp.stack(outs, axis=2)


def pool_branch(u, pool_mix, pool_scale, w_up):
    b, s, _ = u.shape
    m = multiscale_pool(u)
    m = jnp.einsum('bsgc,gcd->bsgd', m, pool_mix.astype(jnp.float32)).reshape(b, s, D_POOL)
    m = (m * pool_scale.astype(jnp.float32)).astype(u.dtype)
    return m @ w_up


def wkv7_scan(r, decay, k, v, kk, a):
    b, s, h, n = r.shape

    def to_chunks(z):
        return jnp.moveaxis(z, 1, 0).reshape(s // CHUNK, CHUNK, b, h, n)

    seqs = tuple(to_chunks(z) for z in (r, decay, k, v, -kk, kk * a))

    def step(state, inp):
        r_t, w_t, k_t, v_t, a_t, b_t = inp
        sa = jnp.einsum('bhvk,bhk->bhv', state, a_t)
        state = (state * w_t[:, :, None, :] + sa[..., None] * b_t[:, :, None, :]
                 + v_t[..., None] * k_t[:, :, None, :])
        return state, jnp.einsum('bhvk,bhk->bhv', state, r_t)

    def chunk_step(state, chunk_inp):
        return lax.scan(step, state, chunk_inp)

    state0 = jnp.zeros((b, h, n, n), jnp.float32)
    _, out = lax.scan(chunk_step, state0, seqs)
    return jnp.moveaxis(out.reshape(s, b, h, n), 0, 1)


def rwkv7_branch(ps, mu, w0, w_decay_up, a0, w_aaa_up, w_gate_up, k_k, k_a, r_k, gn_g, gn_b, w_up):
    b, s, _ = ps.shape
    xs = ps + (token_shift(ps) - ps) * mu
    r, k, v, xw, xa, xg = jnp.split(xs, SHIFT_SPLITS, axis=-1)
    w = -jax.nn.softplus(-(w0 + jnp.tanh(xw) @ w_decay_up)) - 0.5
    decay = jnp.exp(-jnp.exp(w.astype(jnp.float32)))
    a = jax.nn.sigmoid(a0 + xa @ w_aaa_up)
    g = jax.nn.sigmoid(xg) @ w_gate_up

    def heads(z):
        return z.astype(jnp.float32).reshape(b, s, RWKV_HEADS, RWKV_HEAD)

    kk = heads(k * k_k)
    kk = kk / jnp.maximum(jnp.linalg.norm(kk, axis=-1, keepdims=True), 1e-12)
    k = k * (1 + (a - 1) * k_a)
    rh, kh, vh = heads(r), heads(k), heads(v)
    o = wkv7_scan(rh, heads(decay), kh, vh, kk, heads(a))
    o = layer_norm(o, gn_g.reshape(RWKV_HEADS, RWKV_HEAD), gn_b.reshape(RWKV_HEADS, RWKV_HEAD), GN_EPS)
    o = o + jnp.sum(rh * kh * r_k.astype(jnp.float32), axis=-1, keepdims=True) * vh
    o = (o.reshape(b, s, D_RWKV) * g.astype(jnp.float32)).astype(ps.dtype)
    return o @ w_up


def setup_inputs(seed: int = 0) -> dict:
    key = jax.random.key(seed)
    ks = jax.random.split(key, 28)
    L = DEPTH

    def nrm(k, shape, scale):
        return jax.random.normal(k, shape, jnp.float32) * scale

    return {
        "x": nrm(ks[0], (BATCH, SEQ, D_MODEL), 1.0),
        "c": nrm(ks[1], (BATCH, D_MODEL), 1.0),
        "w_ada": nrm(ks[2], (L, D_MODEL, 6 * D_MODEL), 0.5 * D_MODEL ** -0.5),
        "b_ada": nrm(ks[3], (L, 6 * D_MODEL), 0.02),
        "w_in": nrm(ks[4], (L, D_MODEL, D_IN), D_MODEL ** -0.5),
        "mu_shift": jax.random.uniform(ks[5], (L, D_SHIFTED), jnp.float32),
        "pool_mix": nrm(ks[6], (L, POOL_GROUPS, POOL_GC, POOL_GC), POOL_GC ** -0.5),
        "pool_scale": 1.0 + nrm(ks[7], (L, D_POOL), 0.1),
        "w_pool_up": nrm(ks[8], (L, D_POOL, D_MODEL), D_POOL ** -0.5),
        "w0": jax.random.uniform(ks[9], (L, D_RWKV), jnp.float32, -4.0, 1.0),
        "w_decay_up": nrm(ks[10], (L, D_DECAY_LORA, D_RWKV), 0.5 * D_DECAY_LORA ** -0.5),
        "a0": nrm(ks[11], (L, D_RWKV), 0.5),
        "w_aaa_up": nrm(ks[12], (L, D_AAA_LORA, D_RWKV), 0.5 * D_AAA_LORA ** -0.5),
        "w_gate_up": nrm(ks[13], (L, D_GATE_LORA, D_RWKV), D_GATE_LORA ** -0.5),
        "k_k": 0.85 + nrm(ks[14], (L, D_RWKV), 0.05),
        "k_a": 1.0 + nrm(ks[15], (L, D_RWKV), 0.05),
        "r_k": nrm(ks[16], (L, RWKV_HEADS, RWKV_HEAD), 0.1),
        "gn_g": 1.0 + nrm(ks[17], (L, D_RWKV), 0.05),
        "gn_b": nrm(ks[18], (L, D_RWKV), 0.02),
        "w_rwkv_up": nrm(ks[19], (L, D_RWKV, D_MODEL), D_RWKV ** -0.5),
        "w_out": nrm(ks[20], (L, D_MODEL, D_MODEL), BETA * D_MODEL ** -0.5),
        "ln1_g": 1.0 + nrm(ks[21], (L, D_MODEL), 0.05),
        "ln1_b": nrm(ks[22], (L, D_MODEL), 0.02),
        "w_ff1": nrm(ks[23], (L, D_MODEL, D_FF), D_MODEL ** -0.5),
        "w_ff2": nrm(ks[24], (L, D_FF, D_MODEL), BETA * D_FF ** -0.5),
        "ln2_g": 1.0 + nrm(ks[25], (L, D_MODEL), 0.05),
        "ln2_b": nrm(ks[26], (L, D_MODEL), 0.02),
    }


def reference(x, c, w_ada, b_ada, w_in, mu_shift, pool_mix, pool_scale, w_pool_up, w0, w_decay_up,
              a0, w_aaa_up, w_gate_up, k_k, k_a, r_k, gn_g, gn_b, w_rwkv_up, w_out, ln1_g, ln1_b,
              w_ff1, w_ff2, ln2_g, ln2_b):
    cond = jax.nn.silu(c)
    for l in range(DEPTH):
        mod = (cond @ w_ada[l] + b_ada[l])[:, None, :]
        sh1, sc1, gt1, sh2, sc2, gt2 = jnp.split(mod, 6, axis=-1)
        h = x * (1 + sc1) + sh1
        p = h @ w_in[l]
        p_pool, p_rwkv, p_gate = jnp.split(p, (D_POOL, D_POOL + D_SHIFTED), axis=-1)
        y_pool = pool_branch(p_pool, pool_mix[l], pool_scale[l], w_pool_up[l])
        y_rwkv = rwkv7_branch(p_rwkv, mu_shift[l], w0[l], w_decay_up[l], a0[l], w_aaa_up[l],
                              w_gate_up[l], k_k[l], k_a[l], r_k[l], gn_g[l], gn_b[l], w_rwkv_up[l])
        g_pool, g_rwkv = jnp.split(jax.nn.sigmoid(p_gate), 2, axis=-1)
        mixed = (g_pool * y_pool + g_rwkv * y_rwkv) @ w_out[l]
        x = layer_norm(ALPHA * x + (1 + gt1) * mixed, ln1_g[l], ln1_b[l], LN_EPS).astype(x.dtype)
        h = x * (1 + sc2) + sh2
        f = jnp.square(jax.nn.relu(h @ w_ff1[l])) @ w_ff2[l]
        x = layer_norm(ALPHA * x + (1 + gt2) * f, ln2_g[l], ln2_b[l], LN_EPS).astype(x.dtype)
    return x
```

```python
import functools
import math

import jax
import jax.numpy as jnp
from jax import lax
from jax.experimental import pallas as pl
from jax.experimental.pallas import tpu as pltpu

F32 = jnp.float32
BF16 = jnp.bfloat16

D_MODEL = 1024
POOL_WINDOWS = (2, 4, 8, 16)
POOL_GC = 128
D_POOL = 512
HEAD = 64
D_RWKV = 1024
D_DECAY_LORA = 64
D_AAA_LORA = 64
D_GATE_LORA = 160
D_FF = 4096
LN_EPS = 1e-5
GN_EPS = 64e-5
EXP_NEG_HALF = math.exp(-0.5)

CHUNK = 64
GROUP = 256
HEADS_PER_GROUP = GROUP // HEAD
N_GROUPS = D_RWKV // GROUP
POOL_HALO = 16
LORA_W_OFF, LORA_A_OFF, LORA_G_OFF, D_LORA_PAD = 0, 128, 256, 512
D_SHIFT_PAD = 3 * D_RWKV + D_LORA_PAD
D_IN_PAD = D_POOL + D_SHIFT_PAD + 2 * D_MODEL
VMEM_LIMIT_BYTES = 56 * 1024 * 1024


def _dot(a, b):
    return jnp.dot(a, b, preferred_element_type=F32)


def _dot_nt(a, b):
    return lax.dot_general(a, b, (((1,), (1,)), ((), ())), preferred_element_type=F32)


def _dot_tn(a, b):
    return lax.dot_general(a, b, (((0,), (0,)), ((), ())), preferred_element_type=F32)


def _layer_norm(z, g, b, eps):
    mu = jnp.mean(z, axis=-1, keepdims=True)
    d = z - mu
    var = jnp.mean(d * d, axis=-1, keepdims=True)
    return d * lax.rsqrt(var + eps) * g + b


def _mod_kernel(c_ref, w_ref, b_ref, o_ref):
    c = c_ref[...]
    cond = c * jax.nn.sigmoid(c)
    nb = cond.shape[0]
    lhs = jnp.concatenate([cond, jnp.zeros_like(cond)], axis=0).astype(BF16)
    res = _dot(lhs, w_ref[0].astype(BF16))
    o_ref[0] = res[:nb] + b_ref[0]


def _modulation(c, w_ada, b_ada):
    nl, d, d6 = w_ada.shape
    nb = c.shape[0]
    return pl.pallas_call(
        _mod_kernel,
        grid=(nl, d6 // d),
        in_specs=[
            pl.BlockSpec((nb, d), lambda l, n: (0, 0)),
            pl.BlockSpec((1, d, d), lambda l, n: (l, 0, n)),
            pl.BlockSpec((1, 1, d), lambda l, n: (l, 0, n)),
        ],
        out_specs=pl.BlockSpec((1, nb, d), lambda l, n: (l, 0, n)),
        out_shape=jax.ShapeDtypeStruct((nl, nb, d6), F32),
        compiler_params=pltpu.CompilerParams(
            dimension_semantics=("arbitrary", "arbitrary"), vmem_limit_bytes=VMEM_LIMIT_BYTES),
        name="adaln_modulation",
    )(c, w_ada, b_ada.reshape(nl, 1, d6))


def _inproj_kernel(x_ref, mod_ref, w_ref, pool_ref, rkv_ref, lora_ref, gate_ref):
    h = (x_ref[0] * (1.0 + mod_ref[0, 1:2, :]) + mod_ref[0, 0:1, :]).astype(BF16)
    nt = 512
    pool_ref[0] = _dot(h, w_ref[:, 0:D_POOL]).astype(BF16)
    for j in range(0, 3 * D_RWKV, nt):
        rkv_ref[0, :, j:j + nt] = _dot(h, w_ref[:, D_POOL + j:D_POOL + j + nt]).astype(BF16)
    off = D_POOL + 3 * D_RWKV
    lora_ref[0] = _dot(h, w_ref[:, off:off + D_LORA_PAD])
    off += D_LORA_PAD
    for j in range(0, 2 * D_MODEL, nt):
        gate_ref[0, :, j:j + nt] = jax.nn.sigmoid(_dot(h, w_ref[:, off + j:off + j + nt])).astype(BF16)


def _inproj(x, mod_l, w_in_p, tm):
    nb, s, d = x.shape
    return pl.pallas_call(
        _inproj_kernel,
        grid=(nb, s // tm),
        in_specs=[
            pl.BlockSpec((1, tm, d), lambda b, i: (b, i, 0)),
            pl.BlockSpec((1, 6, d), lambda b, i: (b, 0, 0)),
            pl.BlockSpec((d, D_IN_PAD), lambda b, i: (0, 0)),
        ],
        out_specs=[
            pl.BlockSpec((1, tm, D_POOL), lambda b, i: (b, i, 0)),
            pl.BlockSpec((1, tm, 3 * D_RWKV), lambda b, i: (b, i, 0)),
            pl.BlockSpec((1, tm, D_LORA_PAD), lambda b, i: (b, i, 0)),
            pl.BlockSpec((1, tm, 2 * D_MODEL), lambda b, i: (b, i, 0)),
        ],
        out_shape=[
            jax.ShapeDtypeStruct((nb, s, D_POOL), BF16),
            jax.ShapeDtypeStruct((nb, s, 3 * D_RWKV), BF16),
            jax.ShapeDtypeStruct((nb, s, D_LORA_PAD), F32),
            jax.ShapeDtypeStruct((nb, s, 2 * D_MODEL), BF16),
        ],
        compiler_params=pltpu.CompilerParams(
            dimension_semantics=("parallel", "parallel"), vmem_limit_bytes=VMEM_LIMIT_BYTES),
        name="in_projection",
    )(x, mod_l, w_in_p)


def _block_diag(y, bd_mask):
    return jnp.where(bd_mask, jnp.concatenate([y] * HEADS_PER_GROUP, axis=0), 0.0).astype(BF16)


def _mix_kernel(rkv_ref, lora_ref, mu_ref, wd_ref, wa_ref, wg_ref, vec_ref, o_ref, xbuf, s_ref):
    c = pl.program_id(1)

    @pl.when(c == 0)
    def _():
        xbuf[0:8, :] = jnp.zeros((8, D_SHIFT_PAD), F32)
        s_ref[...] = jnp.zeros(s_ref.shape, F32)

    xbuf[8:8 + CHUNK, 0:3 * D_RWKV] = rkv_ref[0].astype(F32)
    xbuf[8:8 + CHUNK, 3 * D_RWKV:] = lora_ref[0]

    def shifted(lo, hi):
        cur = xbuf[8:8 + CHUNK, lo:hi]
        prev = xbuf[7:7 + CHUNK, lo:hi]
        return cur + (prev - cur) * mu_ref[:, lo:hi]

    row_g = lax.broadcasted_iota(jnp.int32, (GROUP, GROUP), 0)
    col_g = lax.broadcasted_iota(jnp.int32, (GROUP, GROUP), 1)
    bd_mask = (row_g // HEAD) == (col_g // HEAD)
    bd_ones = jnp.where(bd_mask, 1.0, 0.0).astype(BF16)
    row_c = lax.broadcasted_iota(jnp.int32, (CHUNK, GROUP), 0)
    col_c = lax.broadcasted_iota(jnp.int32, (CHUNK, GROUP), 1) % HEAD
    strict = col_c < row_c
    incl = col_c <= row_c
    eye_pk = jnp.where(col_c == row_c, 1.0, 0.0)
    tri = jnp.where(lax.broadcasted_iota(jnp.int32, (CHUNK, CHUNK), 1)
                    <= lax.broadcasted_iota(jnp.int32, (CHUNK, CHUNK), 0), 1.0, 0.0).astype(BF16)

    lo = 3 * D_RWKV
    tanh_xw = jnp.tanh(shifted(lo + LORA_W_OFF, lo + LORA_W_OFF + 128)).astype(BF16)
    xa = shifted(lo + LORA_A_OFF, lo + LORA_A_OFF + 128).astype(BF16)
    sig_xg = jax.nn.sigmoid(shifted(lo + LORA_G_OFF, lo + LORA_G_OFF + 256)).astype(BF16)

    for g in range(N_GROUPS):
        c0, c1 = g * GROUP, (g + 1) * GROUP
        w0 = vec_ref[0:1, c0:c1]
        a0 = vec_ref[1:2, c0:c1]
        k_k = vec_ref[2:3, c0:c1]
        k_a = vec_ref[3:4, c0:c1]
        r_k = vec_ref[4:5, c0:c1]
        gn_g = vec_ref[5:6, c0:c1]
        gn_b = vec_ref[6:7, c0:c1]

        r = shifted(c0, c1)
        k = shifted(D_RWKV + c0, D_RWKV + c1)
        v = shifted(2 * D_RWKV + c0, 2 * D_RWKV + c1)

        ld = -EXP_NEG_HALF * jax.nn.sigmoid(w0 + _dot(tanh_xw, wd_ref[:, c0:c1]))
        a = jax.nn.sigmoid(a0 + _dot(xa, wa_ref[:, c0:c1]))
        gate = _dot(sig_xg, wg_ref[:, c0:c1])

        kk = k * k_k
        n2 = _dot((kk * kk).astype(BF16), bd_ones)
        kkn = kk * jnp.minimum(lax.rsqrt(n2), 1e12)
        k2 = k * (1.0 + (a - 1.0) * k_a)

        ld_hi = ld.astype(BF16)
        ld_lo = (ld - ld_hi.astype(F32)).astype(BF16)
        cum = _dot(tri, ld_hi) + _dot(tri, ld_lo)
        w_inc = jnp.exp(cum)
        w_exc = jnp.exp(cum - ld)
        w_inv = jnp.exp(-cum)
        w_last = w_inc[CHUNK - 1:CHUNK, :]

        rt = (r * w_inc).astype(BF16)
        at = (-kkn * w_exc).astype(BF16)
        bt_f = kkn * a * w_inv
        kt_f = k2 * w_inv
        bt = bt_f.astype(BF16)
        kt = kt_f.astype(BF16)
        vb = v.astype(BF16)

        lhs_ar = jnp.concatenate([at, rt], axis=0)
        a1 = _dot_nt(lhs_ar, _block_diag(bt_f, bd_mask))
        a2 = _dot_nt(lhs_ar, _block_diag(kt_f, bd_mask))
        aab = jnp.where(strict, a1[:CHUNK], 0.0)
        arb = jnp.where(incl, a1[CHUNK:], 0.0)
        aak = jnp.where(strict, a2[:CHUNK], 0.0)
        ark = jnp.where(incl, a2[CHUNK:], 0.0)

        t = eye_pk + aab
        p = _dot(aab.astype(BF16), _block_diag(aab, bd_mask))
        for _ in range(4):
            pt = _dot(jnp.concatenate([p, t], axis=0).astype(BF16), _block_diag(p, bd_mask))
            t = t + pt[CHUNK:]
            p = pt[:CHUNK]
        t = t + _dot(t.astype(BF16), _block_diag(p, bd_mask))

        s = s_ref[g]
        xr = _dot_nt(lhs_ar, s.astype(BF16))
        av = _dot(jnp.concatenate([aak, ark], axis=0).astype(BF16), _block_diag(v, bd_mask))
        x = xr[:CHUNK] + av[:CHUNK]
        u = _dot(t.astype(BF16), _block_diag(x, bd_mask))
        ub = u.astype(BF16)
        o = xr[CHUNK:] + av[CHUNK:] + _dot(arb.astype(BF16), _block_diag(u, bd_mask))
        m = _dot_tn(jnp.concatenate([ub, vb], axis=0), jnp.concatenate([bt, kt], axis=0))
        s_ref[g] = jnp.where(bd_mask, s + m, 0.0) * w_last

        mean = _dot(o.astype(BF16), bd_ones) * (1.0 / HEAD)
        d = o - mean
        var = _dot((d * d).astype(BF16), bd_ones) * (1.0 / HEAD)
        on = d * lax.rsqrt(var + GN_EPS) * gn_g + gn_b
        rk = _dot((r * k2 * r_k).astype(BF16), bd_ones)
        o_ref[0, :, c0:c1] = ((on + rk * v) * gate).astype(BF16)

    xbuf[0:8, :] = xbuf[CHUNK:CHUNK + 8, :]


def _mixer(rkv, lora, mu_p, wd, wa, wg, vecs):
    nb, s, _ = rkv.shape
    const = lambda b, c: (0, 0)
    return pl.pallas_call(
        _mix_kernel,
        grid=(nb, s // CHUNK),
        in_specs=[
            pl.BlockSpec((1, CHUNK, 3 * D_RWKV), lambda b, c: (b, c, 0)),
            pl.BlockSpec((1, CHUNK, D_LORA_PAD), lambda b, c: (b, c, 0)),
            pl.BlockSpec((1, D_SHIFT_PAD), const),
            pl.BlockSpec((128, D_RWKV), const),
            pl.BlockSpec((128, D_RWKV), const),
            pl.BlockSpec((256, D_RWKV), const),
            pl.BlockSpec((8, D_RWKV), const),
        ],
        out_specs=pl.BlockSpec((1, CHUNK, D_RWKV), lambda b, c: (b, c, 0)),
        out_shape=jax.ShapeDtypeStruct((nb, s, D_RWKV), BF16),
        scratch_shapes=[
            pltpu.VMEM((8 + CHUNK, D_SHIFT_PAD), F32),
            pltpu.VMEM((N_GROUPS, GROUP, GROUP), F32),
        ],
        compiler_params=pltpu.CompilerParams(
            dimension_semantics=("arbitrary", "arbitrary"), vmem_limit_bytes=VMEM_LIMIT_BYTES),
        name="rwkv7_mixer",
    )(rkv, lora, mu_p, wd, wa, wg, vecs)


def _out_kernel(pool_ref, halo_ref, o_ref, gate_ref, x_ref, mod_ref, pmix_ref, pscale_ref, wpu_ref,
                wru_ref, wout_ref, lng_ref, lnb_ref, xo_ref, pbuf, *, alpha, tm):
    i = pl.program_id(1)
    halo = halo_ref[0].astype(F32)
    pbuf[0:POOL_HALO, :] = jnp.where(i == 0, 0.0, halo)
    pbuf[POOL_HALO:, :] = pool_ref[0].astype(F32)
    t_idx = i * tm + lax.broadcasted_iota(jnp.int32, (tm, POOL_GC), 0)
    ms = []
    for gi, w in enumerate(POOL_WINDOWS):
        c0, c1 = gi * POOL_GC, (gi + 1) * POOL_GC
        u = pbuf[POOL_HALO:, c0:c1]
        acc = u
        for sft in range(1, w):
            acc = acc + pbuf[POOL_HALO - sft:POOL_HALO - sft + tm, c0:c1]
        cnt = jnp.minimum(t_idx + 1, w).astype(F32)
        m = (acc / cnt - u).astype(BF16)
        ms.append(_dot(m, pmix_ref[gi]) * pscale_ref[:, c0:c1])
    m_all = jnp.concatenate(ms, axis=1).astype(BF16)
    y_pool = _dot(m_all, wpu_ref[...])
    y_rwkv = _dot(o_ref[0], wru_ref[...])
    g_pool = gate_ref[0, :, 0:D_MODEL].astype(F32)
    g_rwkv = gate_ref[0, :, D_MODEL:].astype(F32)
    y = (g_pool * y_pool + g_rwkv * y_rwkv).astype(BF16)
    mixed = _dot(y, wout_ref[...])
    z = alpha * x_ref[0] + (1.0 + mod_ref[0, 2:3, :]) * mixed
    xo_ref[0] = _layer_norm(z, lng_ref[...], lnb_ref[...], LN_EPS)


def _out_stage(pool, o, gates, x, mod_l, pmix, pscale, wpu, wru, wout, lng, lnb, tm, alpha):
    nb, s, d = x.shape
    const2 = lambda b, i: (0, 0)
    hpb = tm // POOL_HALO
    return pl.pallas_call(
        functools.partial(_out_kernel, alpha=alpha, tm=tm),
        grid=(nb, s // tm),
        in_specs=[
            pl.BlockSpec((1, tm, D_POOL), lambda b, i: (b, i, 0)),
            pl.BlockSpec((1, POOL_HALO, D_POOL), lambda b, i: (b, jnp.maximum(i * hpb - 1, 0), 0)),
            pl.BlockSpec((1, tm, D_RWKV), lambda b, i: (b, i, 0)),
            pl.BlockSpec((1, tm, 2 * D_MODEL), lambda b, i: (b, i, 0)),
            pl.BlockSpec((1, tm, d), lambda b, i: (b, i, 0)),
            pl.BlockSpec((1, 6, d), lambda b, i: (b, 0, 0)),
            pl.BlockSpec((len(POOL_WINDOWS), POOL_GC, POOL_GC), lambda b, i: (0, 0, 0)),
            pl.BlockSpec((1, D_POOL), const2),
            pl.BlockSpec((D_POOL, d), const2),
            pl.BlockSpec((D_RWKV, d), const2),
            pl.BlockSpec((d, d), const2),
            pl.BlockSpec((1, d), const2),
            pl.BlockSpec((1, d), const2),
        ],
        out_specs=pl.BlockSpec((1, tm, d), lambda b, i: (b, i, 0)),
        out_shape=jax.ShapeDtypeStruct((nb, s, d), F32),
        scratch_shapes=[pltpu.VMEM((POOL_HALO + tm, D_POOL), F32)],
        compiler_params=pltpu.CompilerParams(
            dimension_semantics=("parallel", "parallel"), vmem_limit_bytes=VMEM_LIMIT_BYTES),
        name="merge_out_projection",
    )(pool, pool, o, gates, x, mod_l, pmix, pscale, wpu, wru, wout, lng, lnb)


def _ffn_kernel(x_ref, mod_ref, w1_ref, w2_ref, lng_ref, lnb_ref, xo_ref, hid_ref, *, alpha):
    x = x_ref[0]
    h = (x * (1.0 + mod_ref[0, 4:5, :]) + mod_ref[0, 3:4, :]).astype(BF16)
    nt = 512
    for j in range(0, D_FF, nt):
        a = jnp.maximum(_dot(h, w1_ref[:, j:j + nt]), 0.0)
        hid_ref[:, j:j + nt] = (a * a).astype(BF16)
    f = _dot(hid_ref[...], w2_ref[...])
    z = alpha * x + (1.0 + mod_ref[0, 5:6, :]) * f
    xo_ref[0] = _layer_norm(z, lng_ref[...], lnb_ref[...], LN_EPS)


def _ffn_stage(x, mod_l, w1, w2, lng, lnb, tm, alpha):
    nb, s, d = x.shape
    const2 = lambda b, i: (0, 0)
    return pl.pallas_call(
        functools.partial(_ffn_kernel, alpha=alpha),
        grid=(nb, s // tm),
        in_specs=[
            pl.BlockSpec((1, tm, d), lambda b, i: (b, i, 0)),
            pl.BlockSpec((1, 6, d), lambda b, i: (b, 0, 0)),
            pl.BlockSpec((d, D_FF), const2),
            pl.BlockSpec((D_FF, d), const2),
            pl.BlockSpec((1, d), const2),
            pl.BlockSpec((1, d), const2),
        ],
        out_specs=pl.BlockSpec((1, tm, d), lambda b, i: (b, i, 0)),
        out_shape=jax.ShapeDtypeStruct((nb, s, d), F32),
        scratch_shapes=[pltpu.VMEM((tm, D_FF), BF16)],
        compiler_params=pltpu.CompilerParams(
            dimension_semantics=("parallel", "parallel"), vmem_limit_bytes=VMEM_LIMIT_BYTES),
        name="relu2_mlp",
    )(x, mod_l, w1, w2, lng, lnb)


def _pad_cols(w, width):
    return jnp.pad(w, ((0, 0), (0, width - w.shape[1])))


def _pad_rows(w, height):
    return jnp.pad(w, ((0, height - w.shape[0]), (0, 0)))


def _prep_in_weights(w_in_l, mu_l):
    o_s = D_POOL
    o_l = o_s + 3 * D_RWKV
    o_a = o_l + D_DECAY_LORA
    o_g = o_a + D_AAA_LORA
    o_gate = o_g + D_GATE_LORA
    cols = [
        w_in_l[:, :o_l],
        _pad_cols(w_in_l[:, o_l:o_a], 128),
        _pad_cols(w_in_l[:, o_a:o_g], 128),
        _pad_cols(w_in_l[:, o_g:o_gate], 256),
        w_in_l[:, o_gate:],
    ]
    w_p = jnp.concatenate(cols, axis=1).astype(BF16)
    m = mu_l[None, :]
    r3 = 3 * D_RWKV
    mu_p = jnp.concatenate([
        m[:, :r3],
        _pad_cols(m[:, r3:r3 + D_DECAY_LORA], 128),
        _pad_cols(m[:, r3 + D_DECAY_LORA:r3 + D_DECAY_LORA + D_AAA_LORA], 128),
        _pad_cols(m[:, r3 + D_DECAY_LORA + D_AAA_LORA:], 256),
    ], axis=1)
    return w_p, mu_p


def kernel(x, c, w_ada, b_ada, w_in, mu_shift, pool_mix, pool_scale, w_pool_up, w0, w_decay_up, a0,
           w_aaa_up, w_gate_up, k_k, k_a, r_k, gn_g, gn_b, w_rwkv_up, w_out, ln1_g, ln1_b, w_ff1, w_ff2,
           ln2_g, ln2_b):
    nb, s, d = x.shape
    depth = w_ada.shape[0]
    assert d == D_MODEL and s % CHUNK == 0
    alpha = (2 * depth) ** 0.25
    tm = min(512, s)
    assert s % tm == 0 and tm % POOL_HALO == 0

    mod = _modulation(c, w_ada, b_ada).reshape(depth, nb, 6, d)
    for l in range(depth):
        w_in_p, mu_p = _prep_in_weights(w_in[l], mu_shift[l])
        vecs = jnp.stack([w0[l], a0[l], k_k[l], k_a[l], r_k[l].reshape(-1), gn_g[l], gn_b[l],
                          jnp.zeros((D_RWKV,), F32)], axis=0)
        pool, rkv, lora, gates = _inproj(x, mod[l], w_in_p, tm)
        o = _mixer(rkv, lora, mu_p,
                   _pad_rows(w_decay_up[l], 128).astype(BF16),
                   _pad_rows(w_aaa_up[l], 128).astype(BF16),
                   _pad_rows(w_gate_up[l], 256).astype(BF16), vecs)
        x = _out_stage(pool, o, gates, x, mod[l], pool_mix[l].astype(BF16), pool_scale[l][None, :],
                       w_pool_up[l].astype(BF16), w_rwkv_up[l].astype(BF16), w_out[l].astype(BF16),
                       ln1_g[l][None, :], ln1_b[l][None, :], tm, alpha)
        x = _ffn_stage(x, mod[l], w_ff1[l].astype(BF16), w_ff2[l].astype(BF16),
                       ln2_g[l][None, :], ln2_b[l][None, :], tm, alpha)
    return x
```

```python
import functools
import math

import jax
import jax.numpy as jnp
from jax import lax
from jax.experimental import pallas as pl
from jax.experimental.pallas import tpu as pltpu

F32 = jnp.float32
BF16 = jnp.bfloat16

D_MODEL = 1024
POOL_WINDOWS = (2, 4, 8, 16)
POOL_GC = 128
D_POOL = 512
HEAD = 64
D_RWKV = 1024
D_DECAY_LORA = 64
D_AAA_LORA = 64
D_GATE_LORA = 160
D_FF = 4096
LN_EPS = 1e-5
GN_EPS = 64e-5
EXP_NEG_HALF = math.exp(-0.5)

CHUNK = 64
GROUP = 256
HEADS_PER_GROUP = GROUP // HEAD
N_GROUPS = D_RWKV // GROUP
POOL_HALO = 16
SHIFT_HALO = 8
MIX_BATCH = 4
LORA_W_OFF, LORA_A_OFF, LORA_G_OFF, D_LORA_PAD = 0, 128, 256, 512
D_SHIFT_PAD = 3 * D_RWKV + D_LORA_PAD
D_IN_PAD = D_POOL + D_SHIFT_PAD + 2 * D_MODEL
VMEM_LIMIT_BYTES = 56 * 1024 * 1024


def _dot(a, b):
    return jnp.dot(a, b, preferred_element_type=F32)


def _dot_nt(a, b):
    return lax.dot_general(a, b, (((1,), (1,)), ((), ())), preferred_element_type=F32)


def _dot_tn(a, b):
    return lax.dot_general(a, b, (((0,), (0,)), ((), ())), preferred_element_type=F32)


def _layer_norm(z, g, b, eps):
    mu = jnp.mean(z, axis=-1, keepdims=True)
    d = z - mu
    var = jnp.mean(d * d, axis=-1, keepdims=True)
    return d * lax.rsqrt(var + eps) * g + b


def _mod_kernel(c_ref, w_ref, b_ref, o_ref):
    c = c_ref[...]
    cond = c * jax.nn.sigmoid(c)
    nb = cond.shape[0]
    lhs = jnp.concatenate([cond, jnp.zeros_like(cond)], axis=0).astype(BF16)
    res = _dot(lhs, w_ref[0].astype(BF16))
    o_ref[0] = res[:nb] + b_ref[0]


def _modulation(c, w_ada, b_ada):
    nl, d, d6 = w_ada.shape
    nb = c.shape[0]
    return pl.pallas_call(
        _mod_kernel,
        grid=(nl, d6 // d),
        in_specs=[
            pl.BlockSpec((nb, d), lambda l, n: (0, 0)),
            pl.BlockSpec((1, d, d), lambda l, n: (l, 0, n)),
            pl.BlockSpec((1, 1, d), lambda l, n: (l, 0, n)),
        ],
        out_specs=pl.BlockSpec((1, nb, d), lambda l, n: (l, 0, n)),
        out_shape=jax.ShapeDtypeStruct((nl, nb, d6), F32),
        compiler_params=pltpu.CompilerParams(
            dimension_semantics=("arbitrary", "arbitrary"), vmem_limit_bytes=VMEM_LIMIT_BYTES),
        name="adaln_modulation",
    )(c, w_ada, b_ada.reshape(nl, 1, d6))


def _inproj_kernel(x_ref, halo_ref, mod_ref, w_ref, mu_ref, pool_ref, rkv_ref, lora_ref, gate_ref, *, tm):
    i = pl.program_id(1)
    scale = 1.0 + mod_ref[0, 1:2, :]
    shift = mod_ref[0, 0:1, :]
    h_cur = x_ref[0] * scale + shift
    h_halo = jnp.where(i == 0, 0.0, halo_ref[0] * scale + shift)
    h_ext = jnp.concatenate([h_halo, h_cur], axis=0).astype(BF16)
    h = h_cur.astype(BF16)
    nt = 512
    pool_ref[0] = _dot(h, w_ref[:, 0:D_POOL]).astype(BF16)

    def shifted(j0, j1):
        p = _dot(h_ext, w_ref[:, D_POOL + j0:D_POOL + j1])
        cur = p[SHIFT_HALO:]
        prev = p[SHIFT_HALO - 1:SHIFT_HALO - 1 + tm]
        return cur + (prev - cur) * mu_ref[:, j0:j1]

    for j in range(0, 3 * D_RWKV, nt):
        rkv_ref[0, :, j:j + nt] = shifted(j, j + nt).astype(BF16)
    lo = 3 * D_RWKV
    xs = shifted(lo, lo + D_LORA_PAD)
    lora_ref[0, :, LORA_W_OFF:LORA_A_OFF] = jnp.tanh(xs[:, LORA_W_OFF:LORA_A_OFF]).astype(BF16)
    lora_ref[0, :, LORA_A_OFF:LORA_G_OFF] = xs[:, LORA_A_OFF:LORA_G_OFF].astype(BF16)
    lora_ref[0, :, LORA_G_OFF:] = jax.nn.sigmoid(xs[:, LORA_G_OFF:]).astype(BF16)
    off = D_POOL + D_SHIFT_PAD
    for j in range(0, 2 * D_MODEL, nt):
        gate_ref[0, :, j:j + nt] = jax.nn.sigmoid(_dot(h, w_ref[:, off + j:off + j + nt])).astype(BF16)


def _inproj(x, mod_l, w_in_p, mu_p, tm):
    nb, s, d = x.shape
    hpb = tm // SHIFT_HALO
    return pl.pallas_call(
        functools.partial(_inproj_kernel, tm=tm),
        grid=(nb, s // tm),
        in_specs=[
            pl.BlockSpec((1, tm, d), lambda b, i: (b, i, 0)),
            pl.BlockSpec((1, SHIFT_HALO, d), lambda b, i: (b, jnp.maximum(i * hpb - 1, 0), 0)),
            pl.BlockSpec((1, 6, d), lambda b, i: (b, 0, 0)),
            pl.BlockSpec((d, D_IN_PAD), lambda b, i: (0, 0)),
            pl.BlockSpec((1, D_SHIFT_PAD), lambda b, i: (0, 0)),
        ],
        out_specs=[
            pl.BlockSpec((1, tm, D_POOL), lambda b, i: (b, i, 0)),
            pl.BlockSpec((1, tm, 3 * D_RWKV), lambda b, i: (b, i, 0)),
            pl.BlockSpec((1, tm, D_LORA_PAD), lambda b, i: (b, i, 0)),
            pl.BlockSpec((1, tm, 2 * D_MODEL), lambda b, i: (b, i, 0)),
        ],
        out_shape=[
            jax.ShapeDtypeStruct((nb, s, D_POOL), BF16),
            jax.ShapeDtypeStruct((nb, s, 3 * D_RWKV), BF16),
            jax.ShapeDtypeStruct((nb, s, D_LORA_PAD), BF16),
            jax.ShapeDtypeStruct((nb, s, 2 * D_MODEL), BF16),
        ],
        compiler_params=pltpu.CompilerParams(
            dimension_semantics=("parallel", "parallel"), vmem_limit_bytes=VMEM_LIMIT_BYTES),
        name="in_projection",
    )(x, x, mod_l, w_in_p, mu_p)


def _block_diag(y, bd_mask):
    return jnp.where(bd_mask, jnp.concatenate([y] * HEADS_PER_GROUP, axis=0), 0.0).astype(BF16)


def _mix_kernel(rkv_ref, lora_ref, wd_ref, wa_ref, wg_ref, vec_ref, o_ref, s_ref):
    c = pl.program_id(1)

    @pl.when(c == 0)
    def _():
        s_ref[...] = jnp.zeros(s_ref.shape, F32)

    row_g = lax.broadcasted_iota(jnp.int32, (GROUP, GROUP), 0)
    col_g = lax.broadcasted_iota(jnp.int32, (GROUP, GROUP), 1)
    bd_mask = (row_g // HEAD) == (col_g // HEAD)
    bd_ones = jnp.where(bd_mask, 1.0, 0.0).astype(BF16)
    row_c = lax.broadcasted_iota(jnp.int32, (CHUNK, GROUP), 0)
    col_c = lax.broadcasted_iota(jnp.int32, (CHUNK, GROUP), 1) % HEAD
    strict = col_c < row_c
    incl = col_c <= row_c
    eye_pk = jnp.where(col_c == row_c, 1.0, 0.0)
    tri = jnp.where(lax.broadcasted_iota(jnp.int32, (CHUNK, CHUNK), 1)
                    <= lax.broadcasted_iota(jnp.int32, (CHUNK, CHUNK), 0), 1.0, 0.0).astype(BF16)

    chains = [(b, g) for b in range(MIX_BATCH) for g in range(N_GROUPS)]
    n_ch = len(chains)
    G = range(n_ch)
    cols = [(g * GROUP, (g + 1) * GROUP) for _, g in chains]

    def vec(row, i):
        return vec_ref[row:row + 1, cols[i][0]:cols[i][1]]

    def rows(stacked, i):
        return stacked[i * CHUNK:(i + 1) * CHUNK]

    def seg_sum(parts):
        return _dot(jnp.concatenate([p.astype(BF16) for p in parts], axis=0), bd_ones)

    r = [rkv_ref[b, :, c0:c1].astype(F32) for (b, _), (c0, c1) in zip(chains, cols)]
    k = [rkv_ref[b, :, D_RWKV + c0:D_RWKV + c1].astype(F32) for (b, _), (c0, c1) in zip(chains, cols)]
    v = [rkv_ref[b, :, 2 * D_RWKV + c0:2 * D_RWKV + c1].astype(F32) for (b, _), (c0, c1) in zip(chains, cols)]

    tanh_xw = jnp.concatenate([lora_ref[b, :, LORA_W_OFF:LORA_A_OFF] for b in range(MIX_BATCH)], axis=0)
    xa = jnp.concatenate([lora_ref[b, :, LORA_A_OFF:LORA_G_OFF] for b in range(MIX_BATCH)], axis=0)
    sig_xg = jnp.concatenate([lora_ref[b, :, LORA_G_OFF:] for b in range(MIX_BATCH)], axis=0)
    z_g = [_dot(tanh_xw, wd_ref[:, g * GROUP:(g + 1) * GROUP]) for g in range(N_GROUPS)]
    a_g = [_dot(xa, wa_ref[:, g * GROUP:(g + 1) * GROUP]) for g in range(N_GROUPS)]
    gate_g = [_dot(sig_xg, wg_ref[:, g * GROUP:(g + 1) * GROUP]) for g in range(N_GROUPS)]
    ld = [-EXP_NEG_HALF * jax.nn.sigmoid(vec(0, i) + rows(z_g[g], b)) for i, (b, g) in enumerate(chains)]
    a = [jax.nn.sigmoid(vec(1, i) + rows(a_g[g], b)) for i, (b, g) in enumerate(chains)]
    gate = [rows(gate_g[g], b) for (b, g) in chains]

    kk = [k[i] * vec(2, i) for i in G]
    n2 = seg_sum([kk[i] * kk[i] for i in G])
    kkn = [kk[i] * jnp.minimum(lax.rsqrt(rows(n2, i)), 1e12) for i in G]
    k2 = [k[i] * (1.0 + (a[i] - 1.0) * vec(3, i)) for i in G]

    ld_hi = [ld[i].astype(BF16) for i in G]
    ld_lo = [(ld[i] - ld_hi[i].astype(F32)).astype(BF16) for i in G]
    cum = [_dot(tri, ld_hi[i]) + _dot(tri, ld_lo[i]) for i in G]
    w_inc = [jnp.exp(cum[i]) for i in G]
    w_exc = [jnp.exp(cum[i] - ld[i]) for i in G]
    w_inv = [jnp.exp(-cum[i]) for i in G]

    rt = [(r[i] * w_inc[i]).astype(BF16) for i in G]
    at = [(-kkn[i] * w_exc[i]).astype(BF16) for i in G]
    bt_f = [kkn[i] * a[i] * w_inv[i] for i in G]
    kt_f = [k2[i] * w_inv[i] for i in G]
    lhs_ar = [jnp.concatenate([at[i], rt[i]], axis=0) for i in G]

    a1 = [_dot_nt(lhs_ar[i], _block_diag(bt_f[i], bd_mask)) for i in G]
    a2 = [_dot_nt(lhs_ar[i], _block_diag(kt_f[i], bd_mask)) for i in G]
    aab = [jnp.where(strict, a1[i][:CHUNK], 0.0) for i in G]
    arb = [jnp.where(incl, a1[i][CHUNK:], 0.0) for i in G]
    aak = [jnp.where(strict, a2[i][:CHUNK], 0.0) for i in G]
    ark = [jnp.where(incl, a2[i][CHUNK:], 0.0) for i in G]

    s = [s_ref[b, g] for b, g in chains]
    xr = [_dot_nt(lhs_ar[i], s[i].astype(BF16)) for i in G]
    av = [_dot(jnp.concatenate([aak[i], ark[i]], axis=0).astype(BF16), _block_diag(v[i], bd_mask)) for i in G]
    x = [xr[i][:CHUNK] + av[i][:CHUNK] for i in G]

    t = [eye_pk + aab[i] for i in G]
    p = [_dot(aab[i].astype(BF16), _block_diag(aab[i], bd_mask)) for i in G]
    for _ in range(4):
        pt = [_dot(jnp.concatenate([p[i], t[i]], axis=0).astype(BF16), _block_diag(p[i], bd_mask)) for i in G]
        t = [t[i] + pt[i][CHUNK:] for i in G]
        p = [pt[i][:CHUNK] for i in G]
    t = [t[i] + _dot(t[i].astype(BF16), _block_diag(p[i], bd_mask)) for i in G]

    u = [_dot(t[i].astype(BF16), _block_diag(x[i], bd_mask)) for i in G]
    o = [xr[i][CHUNK:] + av[i][CHUNK:] + _dot(arb[i].astype(BF16), _block_diag(u[i], bd_mask)) for i in G]
    m = [_dot_tn(jnp.concatenate([u[i].astype(BF16), v[i].astype(BF16)], axis=0),
                 jnp.concatenate([bt_f[i].astype(BF16), kt_f[i].astype(BF16)], axis=0)) for i in G]
    for i, (b, g) in enumerate(chains):
        s_ref[b, g] = jnp.where(bd_mask, s[i] + m[i], 0.0) * w_inc[i][CHUNK - 1:CHUNK, :]

    mean = seg_sum(o) * (1.0 / HEAD)
    d = [o[i] - rows(mean, i) for i in G]
    var = seg_sum([d[i] * d[i] for i in G]) * (1.0 / HEAD)
    rk = seg_sum([r[i] * k2[i] * vec(4, i) for i in G])
    for i, (b, g) in enumerate(chains):
        on = d[i] * lax.rsqrt(rows(var, i) + GN_EPS) * vec(5, i) + vec(6, i)
        o_ref[b, :, cols[i][0]:cols[i][1]] = ((on + rows(rk, i) * v[i]) * gate[i]).astype(BF16)


def _mixer(rkv, lora, wd, wa, wg, vecs):
    nb, s, _ = rkv.shape
    const = lambda b, c: (0, 0)
    return pl.pallas_call(
        _mix_kernel,
        grid=(nb // MIX_BATCH, s // CHUNK),
        in_specs=[
            pl.BlockSpec((MIX_BATCH, CHUNK, 3 * D_RWKV), lambda b, c: (b, c, 0)),
            pl.BlockSpec((MIX_BATCH, CHUNK, D_LORA_PAD), lambda b, c: (b, c, 0)),
            pl.BlockSpec((128, D_RWKV), const),
            pl.BlockSpec((128, D_RWKV), const),
            pl.BlockSpec((256, D_RWKV), const),
            pl.BlockSpec((8, D_RWKV), const),
        ],
        out_specs=pl.BlockSpec((MIX_BATCH, CHUNK, D_RWKV), lambda b, c: (b, c, 0)),
        out_shape=jax.ShapeDtypeStruct((nb, s, D_RWKV), BF16),
        scratch_shapes=[pltpu.VMEM((MIX_BATCH, N_GROUPS, GROUP, GROUP), F32)],
        compiler_params=pltpu.CompilerParams(
            dimension_semantics=("arbitrary", "arbitrary"), vmem_limit_bytes=VMEM_LIMIT_BYTES),
        name="rwkv7_mixer",
    )(rkv, lora, wd, wa, wg, vecs)


def _out_kernel(pool_ref, halo_ref, o_ref, gate_ref, x_ref, mod_ref, pmix_ref, pscale_ref, wpu_ref,
                wru_ref, wout_ref, lng_ref, lnb_ref, xo_ref, pbuf, *, alpha, tm):
    i = pl.program_id(1)
    halo = halo_ref[0].astype(F32)
    pbuf[0:POOL_HALO, :] = jnp.where(i == 0, 0.0, halo)
    pbuf[POOL_HALO:, :] = pool_ref[0].astype(F32)
    t_idx = i * tm + lax.broadcasted_iota(jnp.int32, (tm, POOL_GC), 0)
    ms = []
    for gi, w in enumerate(POOL_WINDOWS):
        c0, c1 = gi * POOL_GC, (gi + 1) * POOL_GC
        u = pbuf[POOL_HALO:, c0:c1]
        acc = u
        for sft in range(1, w):
            acc = acc + pbuf[POOL_HALO - sft:POOL_HALO - sft + tm, c0:c1]
        cnt = jnp.minimum(t_idx + 1, w).astype(F32)
        m = (acc / cnt - u).astype(BF16)
        ms.append(_dot(m, pmix_ref[gi]) * pscale_ref[:, c0:c1])
    m_all = jnp.concatenate(ms, axis=1).astype(BF16)
    y_pool = _dot(m_all, wpu_ref[...])
    y_rwkv = _dot(o_ref[0], wru_ref[...])
    g_pool = gate_ref[0, :, 0:D_MODEL].astype(F32)
    g_rwkv = gate_ref[0, :, D_MODEL:].astype(F32)
    y = (g_pool * y_pool + g_rwkv * y_rwkv).astype(BF16)
    mixed = _dot(y, wout_ref[...])
    z = alpha * x_ref[0] + (1.0 + mod_ref[0, 2:3, :]) * mixed
    xo_ref[0] = _layer_norm(z, lng_ref[...], lnb_ref[...], LN_EPS)


def _out_stage(pool, o, gates, x, mod_l, pmix, pscale, wpu, wru, wout, lng, lnb, tm, alpha):
    nb, s, d = x.shape
    const2 = lambda b, i: (0, 0)
    hpb = tm // POOL_HALO
    return pl.pallas_call(
        functools.partial(_out_kernel, alpha=alpha, tm=tm),
        grid=(nb, s // tm),
        in_specs=[
            pl.BlockSpec((1, tm, D_POOL), lambda b, i: (b, i, 0)),
            pl.BlockSpec((1, POOL_HALO, D_POOL), lambda b, i: (b, jnp.maximum(i * hpb - 1, 0), 0)),
            pl.BlockSpec((1, tm, D_RWKV), lambda b, i: (b, i, 0)),
            pl.BlockSpec((1, tm, 2 * D_MODEL), lambda b, i: (b, i, 0)),
            pl.BlockSpec((1, tm, d), lambda b, i: (b, i, 0)),
            pl.BlockSpec((1, 6, d), lambda b, i: (b, 0, 0)),
            pl.BlockSpec((len(POOL_WINDOWS), POOL_GC, POOL_GC), lambda b, i: (0, 0, 0)),
            pl.BlockSpec((1, D_POOL), const2),
            pl.BlockSpec((D_POOL, d), const2),
            pl.BlockSpec((D_RWKV, d), const2),
            pl.BlockSpec((d, d), const2),
            pl.BlockSpec((1, d), const2),
            pl.BlockSpec((1, d), const2),
        ],
        out_specs=pl.BlockSpec((1, tm, d), lambda b, i: (b, i, 0)),
        out_shape=jax.ShapeDtypeStruct((nb, s, d), F32),
        scratch_shapes=[pltpu.VMEM((POOL_HALO + tm, D_POOL), F32)],
        compiler_params=pltpu.CompilerParams(
            dimension_semantics=("parallel", "parallel"), vmem_limit_bytes=VMEM_LIMIT_BYTES),
        name="merge_out_projection",
    )(pool, pool, o, gates, x, mod_l, pmix, pscale, wpu, wru, wout, lng, lnb)


def _ffn_kernel(x_ref, mod_ref, w1_ref, w2_ref, lng_ref, lnb_ref, xo_ref, hid_ref, *, alpha):
    x = x_ref[0]
    h = (x * (1.0 + mod_ref[0, 4:5, :]) + mod_ref[0, 3:4, :]).astype(BF16)
    nt = 512
    for j in range(0, D_FF, nt):
        a = jnp.maximum(_dot(h, w1_ref[:, j:j + nt]), 0.0)
        hid_ref[:, j:j + nt] = (a * a).astype(BF16)
    f = _dot(hid_ref[...], w2_ref[...])
    z = alpha * x + (1.0 + mod_ref[0, 5:6, :]) * f
    xo_ref[0] = _layer_norm(z, lng_ref[...], lnb_ref[...], LN_EPS)


def _ffn_stage(x, mod_l, w1, w2, lng, lnb, tm, alpha):
    nb, s, d = x.shape
    const2 = lambda b, i: (0, 0)
    return pl.pallas_call(
        functools.partial(_ffn_kernel, alpha=alpha),
        grid=(nb, s // tm),
        in_specs=[
            pl.BlockSpec((1, tm, d), lambda b, i: (b, i, 0)),
            pl.BlockSpec((1, 6, d), lambda b, i: (b, 0, 0)),
            pl.BlockSpec((d, D_FF), const2),
            pl.BlockSpec((D_FF, d), const2),
            pl.BlockSpec((1, d), const2),
            pl.BlockSpec((1, d), const2),
        ],
        out_specs=pl.BlockSpec((1, tm, d), lambda b, i: (b, i, 0)),
        out_shape=jax.ShapeDtypeStruct((nb, s, d), F32),
        scratch_shapes=[pltpu.VMEM((tm, D_FF), BF16)],
        compiler_params=pltpu.CompilerParams(
            dimension_semantics=("parallel", "parallel"), vmem_limit_bytes=VMEM_LIMIT_BYTES),
        name="relu2_mlp",
    )(x, mod_l, w1, w2, lng, lnb)


def _pad_cols(w, width):
    return jnp.pad(w, ((0, 0), (0, width - w.shape[1])))


def _pad_rows(w, height):
    return jnp.pad(w, ((0, height - w.shape[0]), (0, 0)))


def _prep_in_weights(w_in_l, mu_l):
    o_s = D_POOL
    o_l = o_s + 3 * D_RWKV
    o_a = o_l + D_DECAY_LORA
    o_g = o_a + D_AAA_LORA
    o_gate = o_g + D_GATE_LORA
    cols = [
        w_in_l[:, :o_l],
        _pad_cols(w_in_l[:, o_l:o_a], 128),
        _pad_cols(w_in_l[:, o_a:o_g], 128),
        _pad_cols(w_in_l[:, o_g:o_gate], 256),
        w_in_l[:, o_gate:],
    ]
    w_p = jnp.concatenate(cols, axis=1).astype(BF16)
    m = mu_l[None, :]
    r3 = 3 * D_RWKV
    mu_p = jnp.concatenate([
        m[:, :r3],
        _pad_cols(m[:, r3:r3 + D_DECAY_LORA], 128),
        _pad_cols(m[:, r3 + D_DECAY_LORA:r3 + D_DECAY_LORA + D_AAA_LORA], 128),
        _pad_cols(m[:, r3 + D_DECAY_LORA + D_AAA_LORA:], 256),
    ], axis=1)
    return w_p, mu_p


def kernel(x, c, w_ada, b_ada, w_in, mu_shift, pool_mix, pool_scale, w_pool_up, w0, w_decay_up, a0,
           w_aaa_up, w_gate_up, k_k, k_a, r_k, gn_g, gn_b, w_rwkv_up, w_out, ln1_g, ln1_b, w_ff1, w_ff2,
           ln2_g, ln2_b):
    nb, s, d = x.shape
    depth = w_ada.shape[0]
    assert d == D_MODEL and s % CHUNK == 0 and nb % MIX_BATCH == 0
    alpha = (2 * depth) ** 0.25
    tm = min(512, s)
    assert s % tm == 0 and tm % POOL_HALO == 0

    mod = _modulation(c, w_ada, b_ada).reshape(depth, nb, 6, d)
    for l in range(depth):
        w_in_p, mu_p = _prep_in_weights(w_in[l], mu_shift[l])
        vecs = jnp.stack([w0[l], a0[l], k_k[l], k_a[l], r_k[l].reshape(-1), gn_g[l], gn_b[l],
                          jnp.zeros((D_RWKV,), F32)], axis=0)
        pool, rkv, lora, gates = _inproj(x, mod[l], w_in_p, mu_p, tm)
        o = _mixer(rkv, lora,
                   _pad_rows(w_decay_up[l], 128).astype(BF16),
                   _pad_rows(w_aaa_up[l], 128).astype(BF16),
                   _pad_rows(w_gate_up[l], 256).astype(BF16), vecs)
        x = _out_stage(pool, o, gates, x, mod[l], pool_mix[l].astype(BF16), pool_scale[l][None, :],
                       w_pool_up[l].astype(BF16), w_rwkv_up[l].astype(BF16), w_out[l].astype(BF16),
                       ln1_g[l][None, :], ln1_b[l][None, :], tm, alpha)
        x = _ffn_stage(x, mod[l], w_ff1[l].astype(BF16), w_ff2[l].astype(BF16),
                       ln2_g[l][None, :], ln2_b[l][None, :], tm, alpha)
    return x
```

```python
import functools
import math

import jax
import jax.numpy as jnp
from jax import lax
from jax.experimental import pallas as pl
from jax.experimental.pallas import tpu as pltpu

F32 = jnp.float32
BF16 = jnp.bfloat16

D_MODEL = 1024
POOL_WINDOWS = (2, 4, 8, 16)
POOL_GC = 128
D_POOL = 512
HEAD = 64
D_RWKV = 1024
D_DECAY_LORA = 64
D_AAA_LORA = 64
D_GATE_LORA = 160
D_FF = 4096
LN_EPS = 1e-5
GN_EPS = 64e-5
EXP_NEG_HALF = math.exp(-0.5)

CHUNK = 64
GROUP = 256
HEADS_PER_GROUP = GROUP // HEAD
N_GROUPS = D_RWKV // GROUP
POOL_HALO = 16
SHIFT_HALO = 8
MIX_BATCH = 4
IN_COL_TILE = 512
OUT_SUB = 2
LORA_W_OFF, LORA_A_OFF, LORA_G_OFF, D_LORA_PAD = 0, 128, 256, 512
D_SHIFT_PAD = 3 * D_RWKV + D_LORA_PAD
D_IN_PAD = D_POOL + D_SHIFT_PAD + 2 * D_MODEL
VMEM_LIMIT_BYTES = 56 * 1024 * 1024


def _dot(a, b):
    return jnp.dot(a, b, preferred_element_type=F32)


def _dot_nt(a, b):
    return lax.dot_general(a, b, (((1,), (1,)), ((), ())), preferred_element_type=F32)


def _dot_tn(a, b):
    return lax.dot_general(a, b, (((0,), (0,)), ((), ())), preferred_element_type=F32)


def _layer_norm(z, g, b, eps):
    mu = jnp.mean(z, axis=-1, keepdims=True)
    d = z - mu
    var = jnp.mean(d * d, axis=-1, keepdims=True)
    return d * lax.rsqrt(var + eps) * g + b


def _mod_kernel(c_ref, w_ref, b_ref, o_ref):
    c = c_ref[...]
    cond = c * jax.nn.sigmoid(c)
    nb = cond.shape[0]
    lhs = jnp.concatenate([cond, jnp.zeros_like(cond)], axis=0).astype(BF16)
    res = _dot(lhs, w_ref[0].astype(BF16))
    o_ref[0] = res[:nb] + b_ref[0]


def _modulation(c, w_ada, b_ada):
    nl, d, d6 = w_ada.shape
    nb = c.shape[0]
    return pl.pallas_call(
        _mod_kernel,
        grid=(nl, d6 // d),
        in_specs=[
            pl.BlockSpec((nb, d), lambda l, n: (0, 0)),
            pl.BlockSpec((1, d, d), lambda l, n: (l, 0, n)),
            pl.BlockSpec((1, 1, d), lambda l, n: (l, 0, n)),
        ],
        out_specs=pl.BlockSpec((1, nb, d), lambda l, n: (l, 0, n)),
        out_shape=jax.ShapeDtypeStruct((nl, nb, d6), F32),
        compiler_params=pltpu.CompilerParams(
            dimension_semantics=("arbitrary", "arbitrary"), vmem_limit_bytes=VMEM_LIMIT_BYTES),
        name="adaln_modulation",
    )(c, w_ada, b_ada.reshape(nl, 1, d6))


def _inproj_kernel(x_ref, halo_ref, mod_ref, w_ref, mu_ref, pool_ref, rkv_ref, lora_ref, gate_ref, *, tm):
    i = pl.program_id(1)
    scale = 1.0 + mod_ref[0, 0, 1:2, :]
    shift = mod_ref[0, 0, 0:1, :]
    h_cur = x_ref[0] * scale + shift
    h_halo = jnp.where(i == 0, 0.0, halo_ref[0] * scale + shift)
    h_ext = jnp.concatenate([h_halo, h_cur], axis=0).astype(BF16)
    h = h_cur.astype(BF16)
    nt = IN_COL_TILE

    def shifted(p, j0, j1):
        cur = p[SHIFT_HALO:]
        prev = pltpu.roll(p, 1, 0)[SHIFT_HALO:]
        return cur + (prev - cur) * mu_ref[0, :, j0:j1]

    def pool_out(p):
        pool_ref[0] = p.astype(BF16)

    def rkv_out(p, j):
        rkv_ref[0, :, j:j + nt] = shifted(p, j, j + nt).astype(BF16)

    def lora_out(p):
        xs = shifted(p, 3 * D_RWKV, D_SHIFT_PAD)
        lora_ref[0, :, LORA_W_OFF:LORA_A_OFF] = jnp.tanh(xs[:, LORA_W_OFF:LORA_A_OFF]).astype(BF16)
        lora_ref[0, :, LORA_A_OFF:LORA_G_OFF] = xs[:, LORA_A_OFF:LORA_G_OFF].astype(BF16)
        lora_ref[0, :, LORA_G_OFF:] = jax.nn.sigmoid(xs[:, LORA_G_OFF:]).astype(BF16)

    def gate_out(p, j):
        gate_ref[0, :, j:j + nt] = jax.nn.sigmoid(p).astype(BF16)

    tasks = [(h, 0, D_POOL, pool_out)]
    tasks += [(h_ext, D_POOL + j, nt, functools.partial(rkv_out, j=j)) for j in range(0, 3 * D_RWKV, nt)]
    tasks += [(h_ext, D_POOL + 3 * D_RWKV, D_LORA_PAD, lora_out)]
    off = D_POOL + D_SHIFT_PAD
    tasks += [(h, off + j, nt, functools.partial(gate_out, j=j)) for j in range(0, 2 * D_MODEL, nt)]
    pending = None
    for lhs, c0, width, epilogue in tasks:
        p = _dot(lhs, w_ref[0, :, c0:c0 + width])
        if pending is not None:
            pending[1](pending[0])
        pending = (p, epilogue)
    pending[1](pending[0])


def _inproj(x, mod, w_in_p, mu_p, l, tm):
    nb, s, d = x.shape
    hpb = tm // SHIFT_HALO
    return pl.pallas_call(
        functools.partial(_inproj_kernel, tm=tm),
        grid=(nb, s // tm),
        in_specs=[
            pl.BlockSpec((1, tm, d), lambda b, i: (b, i, 0)),
            pl.BlockSpec((1, SHIFT_HALO, d), lambda b, i: (b, jnp.maximum(i * hpb - 1, 0), 0)),
            pl.BlockSpec((1, 1, 6, d), lambda b, i: (l, b, 0, 0)),
            pl.BlockSpec((1, d, D_IN_PAD), lambda b, i: (l, 0, 0)),
            pl.BlockSpec((1, 1, D_SHIFT_PAD), lambda b, i: (l, 0, 0)),
        ],
        out_specs=[
            pl.BlockSpec((1, tm, D_POOL), lambda b, i: (b, i, 0)),
            pl.BlockSpec((1, tm, 3 * D_RWKV), lambda b, i: (b, i, 0)),
            pl.BlockSpec((1, tm, D_LORA_PAD), lambda b, i: (b, i, 0)),
            pl.BlockSpec((1, tm, 2 * D_MODEL), lambda b, i: (b, i, 0)),
        ],
        out_shape=[
            jax.ShapeDtypeStruct((nb, s, D_POOL), BF16),
            jax.ShapeDtypeStruct((nb, s, 3 * D_RWKV), BF16),
            jax.ShapeDtypeStruct((nb, s, D_LORA_PAD), BF16),
            jax.ShapeDtypeStruct((nb, s, 2 * D_MODEL), BF16),
        ],
        compiler_params=pltpu.CompilerParams(
            dimension_semantics=("parallel", "parallel"), vmem_limit_bytes=VMEM_LIMIT_BYTES),
        name="in_projection",
    )(x, x, mod, w_in_p, mu_p)


def _block_diag(y, bd_mask):
    return jnp.where(bd_mask, jnp.concatenate([y] * HEADS_PER_GROUP, axis=0), 0.0).astype(BF16)


def _mix_kernel(rkv_ref, lora_ref, wd_ref, wa_ref, wg_ref, vec_ref, o_ref, s_ref):
    c = pl.program_id(1)

    @pl.when(c == 0)
    def _():
        s_ref[...] = jnp.zeros(s_ref.shape, F32)

    row_g = lax.broadcasted_iota(jnp.int32, (GROUP, GROUP), 0)
    col_g = lax.broadcasted_iota(jnp.int32, (GROUP, GROUP), 1)
    bd_mask = (row_g // HEAD) == (col_g // HEAD)
    bd_ones = jnp.where(bd_mask, 1.0, 0.0).astype(BF16)
    row_c = lax.broadcasted_iota(jnp.int32, (CHUNK, GROUP), 0)
    col_c = lax.broadcasted_iota(jnp.int32, (CHUNK, GROUP), 1) % HEAD
    strict = col_c < row_c
    incl = col_c <= row_c
    eye_pk = jnp.where(col_c == row_c, 1.0, 0.0)
    tri2 = jnp.where(lax.broadcasted_iota(jnp.int32, (CHUNK, 2 * CHUNK), 1) % CHUNK
                     <= lax.broadcasted_iota(jnp.int32, (CHUNK, 2 * CHUNK), 0), 1.0, 0.0).astype(BF16)

    chains = [(b, g) for b in range(MIX_BATCH) for g in range(N_GROUPS)]
    n_ch = len(chains)
    G = range(n_ch)
    cols = [(g * GROUP, (g + 1) * GROUP) for _, g in chains]

    def vec(row, i):
        return vec_ref[0, row:row + 1, cols[i][0]:cols[i][1]]

    def rows(stacked, i):
        return stacked[i * CHUNK:(i + 1) * CHUNK]

    def seg_sum(parts):
        return _dot(jnp.concatenate([p.astype(BF16) for p in parts], axis=0), bd_ones)

    r = [rkv_ref[b, :, c0:c1].astype(F32) for (b, _), (c0, c1) in zip(chains, cols)]
    k = [rkv_ref[b, :, D_RWKV + c0:D_RWKV + c1].astype(F32) for (b, _), (c0, c1) in zip(chains, cols)]
    v = [rkv_ref[b, :, 2 * D_RWKV + c0:2 * D_RWKV + c1].astype(F32) for (b, _), (c0, c1) in zip(chains, cols)]

    tanh_xw = jnp.concatenate([lora_ref[b, :, LORA_W_OFF:LORA_A_OFF] for b in range(MIX_BATCH)], axis=0)
    xa = jnp.concatenate([lora_ref[b, :, LORA_A_OFF:LORA_G_OFF] for b in range(MIX_BATCH)], axis=0)
    sig_xg = jnp.concatenate([lora_ref[b, :, LORA_G_OFF:] for b in range(MIX_BATCH)], axis=0)
    z_g = [_dot(tanh_xw, wd_ref[0, :, g * GROUP:(g + 1) * GROUP]) for g in range(N_GROUPS)]
    a_g = [_dot(xa, wa_ref[0, :, g * GROUP:(g + 1) * GROUP]) for g in range(N_GROUPS)]
    gate_g = [_dot(sig_xg, wg_ref[0, :, g * GROUP:(g + 1) * GROUP]) for g in range(N_GROUPS)]
    ld = [-EXP_NEG_HALF * jax.nn.sigmoid(vec(0, i) + rows(z_g[g], b)) for i, (b, g) in enumerate(chains)]
    a = [jax.nn.sigmoid(vec(1, i) + rows(a_g[g], b)) for i, (b, g) in enumerate(chains)]
    gate = [rows(gate_g[g], b) for (b, g) in chains]

    kk = [k[i] * vec(2, i) for i in G]
    n2 = seg_sum([kk[i] * kk[i] for i in G])
    kkn = [kk[i] * jnp.minimum(lax.rsqrt(rows(n2, i)), 1e12) for i in G]
    k2 = [k[i] * (1.0 + (a[i] - 1.0) * vec(3, i)) for i in G]

    ld_hi = [ld[i].astype(BF16) for i in G]
    ld_lo = [(ld[i] - ld_hi[i].astype(F32)).astype(BF16) for i in G]
    cum = [_dot(tri2, jnp.concatenate([ld_hi[i], ld_lo[i]], axis=0)) for i in G]
    w_inc = [jnp.exp(cum[i]) for i in G]
    w_exc = [jnp.exp(cum[i] - ld[i]) for i in G]
    w_inv = [jnp.exp(-cum[i]) for i in G]

    rt = [(r[i] * w_inc[i]).astype(BF16) for i in G]
    at = [(-kkn[i] * w_exc[i]).astype(BF16) for i in G]
    bt_f = [kkn[i] * a[i] * w_inv[i] for i in G]
    kt_f = [k2[i] * w_inv[i] for i in G]
    lhs_ar = [jnp.concatenate([at[i], rt[i]], axis=0) for i in G]

    a1 = [_dot_nt(lhs_ar[i], _block_diag(bt_f[i], bd_mask)) for i in G]
    a2 = [_dot_nt(lhs_ar[i], _block_diag(kt_f[i], bd_mask)) for i in G]
    aab = [jnp.where(strict, a1[i][:CHUNK], 0.0) for i in G]
    arb = [jnp.where(incl, a1[i][CHUNK:], 0.0) for i in G]
    aak = [jnp.where(strict, a2[i][:CHUNK], 0.0) for i in G]
    ark = [jnp.where(incl, a2[i][CHUNK:], 0.0) for i in G]

    s = [s_ref[b, g] for b, g in chains]
    xr = [_dot_nt(lhs_ar[i], jnp.where(bd_mask, s[i], 0.0).astype(BF16)) for i in G]
    av = [_dot(jnp.concatenate([aak[i], ark[i]], axis=0).astype(BF16), _block_diag(v[i], bd_mask)) for i in G]
    x = [xr[i][:CHUNK] + av[i][:CHUNK] for i in G]

    t = [eye_pk + aab[i] for i in G]
    p = [_dot(aab[i].astype(BF16), _block_diag(aab[i], bd_mask)) for i in G]
    for _ in range(4):
        pt = [_dot(jnp.concatenate([p[i], t[i]], axis=0).astype(BF16), _block_diag(p[i], bd_mask)) for i in G]
        t = [t[i] + pt[i][CHUNK:] for i in G]
        p = [pt[i][:CHUNK] for i in G]
    t = [t[i] + _dot(t[i].astype(BF16), _block_diag(p[i], bd_mask)) for i in G]

    u = [_dot(t[i].astype(BF16), _block_diag(x[i], bd_mask)) for i in G]
    o = [xr[i][CHUNK:] + av[i][CHUNK:] + _dot(arb[i].astype(BF16), _block_diag(u[i], bd_mask)) for i in G]
    m = [_dot_tn(jnp.concatenate([u[i].astype(BF16), v[i].astype(BF16)], axis=0),
                 jnp.concatenate([bt_f[i].astype(BF16), kt_f[i].astype(BF16)], axis=0)) for i in G]
    for i, (b, g) in enumerate(chains):
        s_ref[b, g] = (s[i] + m[i]) * w_inc[i][CHUNK - 1:CHUNK, :]

    mean = seg_sum(o) * (1.0 / HEAD)
    d = [o[i] - rows(mean, i) for i in G]
    var = seg_sum([d[i] * d[i] for i in G]) * (1.0 / HEAD)
    rk = seg_sum([r[i] * k2[i] * vec(4, i) for i in G])
    for i, (b, g) in enumerate(chains):
        on = d[i] * lax.rsqrt(rows(var, i) + GN_EPS) * vec(5, i) + vec(6, i)
        o_ref[b, :, cols[i][0]:cols[i][1]] = ((on + rows(rk, i) * v[i]) * gate[i]).astype(BF16)


def _mixer(rkv, lora, wd, wa, wg, vecs, l):
    nb, s, _ = rkv.shape
    layer = lambda b, c: (l, 0, 0)
    return pl.pallas_call(
        _mix_kernel,
        grid=(nb // MIX_BATCH, s // CHUNK),
        in_specs=[
            pl.BlockSpec((MIX_BATCH, CHUNK, 3 * D_RWKV), lambda b, c: (b, c, 0)),
            pl.BlockSpec((MIX_BATCH, CHUNK, D_LORA_PAD), lambda b, c: (b, c, 0)),
            pl.BlockSpec((1, 128, D_RWKV), layer),
            pl.BlockSpec((1, 128, D_RWKV), layer),
            pl.BlockSpec((1, 256, D_RWKV), layer),
            pl.BlockSpec((1, 8, D_RWKV), layer),
        ],
        out_specs=pl.BlockSpec((MIX_BATCH, CHUNK, D_RWKV), lambda b, c: (b, c, 0)),
        out_shape=jax.ShapeDtypeStruct((nb, s, D_RWKV), BF16),
        scratch_shapes=[pltpu.VMEM((MIX_BATCH, N_GROUPS, GROUP, GROUP), F32)],
        compiler_params=pltpu.CompilerParams(
            dimension_semantics=("arbitrary", "arbitrary"), vmem_limit_bytes=VMEM_LIMIT_BYTES),
        name="rwkv7_mixer",
    )(rkv, lora, wd, wa, wg, vecs)


def _out_kernel(pool_ref, halo_ref, o_ref, gate_ref, x_ref, mod_ref, pmix_ref, pscale_ref, wpu_ref,
                wru_ref, wout_ref, lng_ref, lnb_ref, xo_ref, *, alpha, tm):
    i = pl.program_id(1)
    halo = jnp.where(i == 0, 0.0, halo_ref[0].astype(F32))
    ext = jnp.concatenate([halo, pool_ref[0].astype(F32)], axis=0)
    t_idx = i * tm + lax.broadcasted_iota(jnp.int32, (tm, POOL_GC), 0)
    ms = []
    for gi, w in enumerate(POOL_WINDOWS):
        e = ext[:, gi * POOL_GC:(gi + 1) * POOL_GC]
        acc, span = e, 1
        while span < w:
            acc = acc + pltpu.roll(acc, span, 0)
            span *= 2
        cnt = jnp.minimum(t_idx + 1, w).astype(F32)
        ms.append((acc[POOL_HALO:] / cnt - e[POOL_HALO:]).astype(BF16))

    rs = [slice(r * (tm // OUT_SUB), (r + 1) * (tm // OUT_SUB)) for r in range(OUT_SUB)]
    n_pw = len(POOL_WINDOWS)
    mixed_pool = [[_dot(ms[gi][r], pmix_ref[0, gi]) * pscale_ref[0, :, gi * POOL_GC:(gi + 1) * POOL_GC]
                   for gi in range(n_pw)] for r in rs]
    m_all = [jnp.concatenate(mp, axis=1).astype(BF16) for mp in mixed_pool]
    y_pool = [_dot(m, wpu_ref[0]) for m in m_all]
    y_rwkv = [_dot(o_ref[0, r, :], wru_ref[0]) for r in rs]
    y = [(gate_ref[0, r, 0:D_MODEL].astype(F32) * yp + gate_ref[0, r, D_MODEL:].astype(F32) * yr).astype(BF16)
         for r, yp, yr in zip(rs, y_pool, y_rwkv)]
    mixed = [_dot(yy, wout_ref[0]) for yy in y]
    for r, mx in zip(rs, mixed):
        z = alpha * x_ref[0, r, :] + (1.0 + mod_ref[0, 0, 2:3, :]) * mx
        xo_ref[0, r, :] = _layer_norm(z, lng_ref[0], lnb_ref[0], LN_EPS)


def _out_stage(pool, o, gates, x, mod, pmix, pscale, wpu, wru, wout, lng, lnb, l, tm, alpha):
    nb, s, d = x.shape
    layer = lambda b, i: (l, 0, 0)
    hpb = tm // POOL_HALO
    return pl.pallas_call(
        functools.partial(_out_kernel, alpha=alpha, tm=tm),
        grid=(nb, s // tm),
        in_specs=[
            pl.BlockSpec((1, tm, D_POOL), lambda b, i: (b, i, 0)),
            pl.BlockSpec((1, POOL_HALO, D_POOL), lambda b, i: (b, jnp.maximum(i * hpb - 1, 0), 0)),
            pl.BlockSpec((1, tm, D_RWKV), lambda b, i: (b, i, 0)),
            pl.BlockSpec((1, tm, 2 * D_MODEL), lambda b, i: (b, i, 0)),
            pl.BlockSpec((1, tm, d), lambda b, i: (b, i, 0)),
            pl.BlockSpec((1, 1, 6, d), lambda b, i: (l, b, 0, 0)),
            pl.BlockSpec((1, len(POOL_WINDOWS), POOL_GC, POOL_GC), lambda b, i: (l, 0, 0, 0)),
            pl.BlockSpec((1, 1, D_POOL), layer),
            pl.BlockSpec((1, D_POOL, d), layer),
            pl.BlockSpec((1, D_RWKV, d), layer),
            pl.BlockSpec((1, d, d), layer),
            pl.BlockSpec((1, 1, d), layer),
            pl.BlockSpec((1, 1, d), layer),
        ],
        out_specs=pl.BlockSpec((1, tm, d), lambda b, i: (b, i, 0)),
        out_shape=jax.ShapeDtypeStruct((nb, s, d), F32),
        compiler_params=pltpu.CompilerParams(
            dimension_semantics=("parallel", "parallel"), vmem_limit_bytes=VMEM_LIMIT_BYTES),
        name="merge_out_projection",
    )(pool, pool, o, gates, x, mod, pmix, pscale, wpu, wru, wout, lng, lnb)


def _ffn_kernel(x_ref, mod_ref, w1_ref, w2_ref, lng_ref, lnb_ref, xo_ref, hid_ref, *, alpha):
    x = x_ref[0]
    h = (x * (1.0 + mod_ref[0, 0, 4:5, :]) + mod_ref[0, 0, 3:4, :]).astype(BF16)
    nt = 512
    for j in range(0, D_FF, nt):
        a = jnp.maximum(_dot(h, w1_ref[0, :, j:j + nt]), 0.0)
        hid_ref[:, j:j + nt] = (a * a).astype(BF16)
    f = _dot(hid_ref[...], w2_ref[0])
    z = alpha * x + (1.0 + mod_ref[0, 0, 5:6, :]) * f
    xo_ref[0] = _layer_norm(z, lng_ref[0], lnb_ref[0], LN_EPS)


def _ffn_stage(x, mod, w1, w2, lng, lnb, l, tm, alpha):
    nb, s, d = x.shape
    layer = lambda b, i: (l, 0, 0)
    return pl.pallas_call(
        functools.partial(_ffn_kernel, alpha=alpha),
        grid=(nb, s // tm),
        in_specs=[
            pl.BlockSpec((1, tm, d), lambda b, i: (b, i, 0)),
            pl.BlockSpec((1, 1, 6, d), lambda b, i: (l, b, 0, 0)),
            pl.BlockSpec((1, d, D_FF), layer),
            pl.BlockSpec((1, D_FF, d), layer),
            pl.BlockSpec((1, 1, d), layer),
            pl.BlockSpec((1, 1, d), layer),
        ],
        out_specs=pl.BlockSpec((1, tm, d), lambda b, i: (b, i, 0)),
        out_shape=jax.ShapeDtypeStruct((nb, s, d), F32),
        scratch_shapes=[pltpu.VMEM((tm, D_FF), BF16)],
        compiler_params=pltpu.CompilerParams(
            dimension_semantics=("parallel", "parallel"), vmem_limit_bytes=VMEM_LIMIT_BYTES),
        name="relu2_mlp",
    )(x, mod, w1, w2, lng, lnb)


def _pad_last(w, width):
    return jnp.pad(w, [(0, 0)] * (w.ndim - 1) + [(0, width - w.shape[-1])])


def _pad_rows(w, height):
    return jnp.pad(w, ((0, 0), (0, height - w.shape[1]), (0, 0)))


def _prep_in_weights(w_in, mu_shift):
    o_l = D_POOL + 3 * D_RWKV
    o_a = o_l + D_DECAY_LORA
    o_g = o_a + D_AAA_LORA
    o_gate = o_g + D_GATE_LORA
    w_p = jnp.concatenate([
        w_in[..., :o_l],
        _pad_last(w_in[..., o_l:o_a], LORA_A_OFF - LORA_W_OFF),
        _pad_last(w_in[..., o_a:o_g], LORA_G_OFF - LORA_A_OFF),
        _pad_last(w_in[..., o_g:o_gate], D_LORA_PAD - LORA_G_OFF),
        w_in[..., o_gate:],
    ], axis=-1).astype(BF16)
    m = mu_shift[:, None, :]
    r3 = 3 * D_RWKV
    mu_p = jnp.concatenate([
        m[..., :r3],
        _pad_last(m[..., r3:r3 + D_DECAY_LORA], LORA_A_OFF - LORA_W_OFF),
        _pad_last(m[..., r3 + D_DECAY_LORA:r3 + D_DECAY_LORA + D_AAA_LORA], LORA_G_OFF - LORA_A_OFF),
        _pad_last(m[..., r3 + D_DECAY_LORA + D_AAA_LORA:], D_LORA_PAD - LORA_G_OFF),
    ], axis=-1)
    return w_p, mu_p


def kernel(x, c, w_ada, b_ada, w_in, mu_shift, pool_mix, pool_scale, w_pool_up, w0, w_decay_up, a0,
           w_aaa_up, w_gate_up, k_k, k_a, r_k, gn_g, gn_b, w_rwkv_up, w_out, ln1_g, ln1_b, w_ff1, w_ff2,
           ln2_g, ln2_b):
    nb, s, d = x.shape
    depth = w_ada.shape[0]
    assert d == D_MODEL and s % CHUNK == 0 and nb % MIX_BATCH == 0
    alpha = (2 * depth) ** 0.25
    tm = min(512, s)
    assert s % tm == 0 and tm % POOL_HALO == 0

    mod = _modulation(c, w_ada, b_ada).reshape(depth, nb, 6, d)
    w_in_p, mu_p = _prep_in_weights(w_in, mu_shift)
    vecs = jnp.stack([w0, a0, k_k, k_a, r_k.reshape(depth, D_RWKV), gn_g, gn_b, jnp.zeros_like(w0)], axis=1)
    wd = _pad_rows(w_decay_up, LORA_A_OFF - LORA_W_OFF).astype(BF16)
    wa = _pad_rows(w_aaa_up, LORA_G_OFF - LORA_A_OFF).astype(BF16)
    wg = _pad_rows(w_gate_up, D_LORA_PAD - LORA_G_OFF).astype(BF16)
    pmix, wpu, wru, wout = (w.astype(BF16) for w in (pool_mix, w_pool_up, w_rwkv_up, w_out))
    w1, w2 = w_ff1.astype(BF16), w_ff2.astype(BF16)
    pscale, g1, b1, g2, b2 = (v[:, None, :] for v in (pool_scale, ln1_g, ln1_b, ln2_g, ln2_b))

    for l in range(depth):
        pool, rkv, lora, gates = _inproj(x, mod, w_in_p, mu_p, l, tm)
        o = _mixer(rkv, lora, wd, wa, wg, vecs, l)
        x = _out_stage(pool, o, gates, x, mod, pmix, pscale, wpu, wru, wout, g1, b1, l, tm, alpha)
        x = _ffn_stage(x, mod, w1, w2, g2, b2, l, tm, alpha)
    return x
```

```python
import functools
import math

import jax
import jax.numpy as jnp
from jax import lax
from jax.experimental import pallas as pl
from jax.experimental.pallas import tpu as pltpu

F32 = jnp.float32
BF16 = jnp.bfloat16

D_MODEL = 1024
POOL_WINDOWS = (2, 4, 8, 16)
POOL_GC = 128
D_POOL = 512
HEAD = 64
D_RWKV = 1024
D_DECAY_LORA = 64
D_AAA_LORA = 64
D_GATE_LORA = 160
D_FF = 4096
LN_EPS = 1e-5
GN_EPS = 64e-5
EXP_NEG_HALF = math.exp(-0.5)

CHUNK = 64
GROUP = 256
HEADS_PER_GROUP = GROUP // HEAD
N_GROUPS = D_RWKV // GROUP
POOL_HALO = 16
SHIFT_HALO = 8
ROW_TILE = 1024
MIX_BATCH = 8
IN_COL_TILE = 512
OUT_SUB = 4
LORA_W_OFF, LORA_A_OFF, LORA_G_OFF, D_LORA_PAD = 0, 128, 256, 512
D_SHIFT_PAD = 3 * D_RWKV + D_LORA_PAD
D_IN_PAD = D_POOL + D_SHIFT_PAD + 2 * D_MODEL
VMEM_LIMIT_BYTES = 56 * 1024 * 1024


def _dot(a, b):
    return jnp.dot(a, b, preferred_element_type=F32)


def _dot_nt(a, b):
    return lax.dot_general(a, b, (((1,), (1,)), ((), ())), preferred_element_type=F32)


def _dot_tn(a, b):
    return lax.dot_general(a, b, (((0,), (0,)), ((), ())), preferred_element_type=F32)


def _layer_norm(z, g, b, eps):
    mu = jnp.mean(z, axis=-1, keepdims=True)
    d = z - mu
    var = jnp.mean(d * d, axis=-1, keepdims=True)
    return d * lax.rsqrt(var + eps) * g + b


def _mod_kernel(c_ref, w_ref, b_ref, o_ref):
    c = c_ref[...]
    cond = c * jax.nn.sigmoid(c)
    nb = cond.shape[0]
    lhs = jnp.concatenate([cond, jnp.zeros_like(cond)], axis=0).astype(BF16)
    res = _dot(lhs, w_ref[0].astype(BF16))
    o_ref[0] = res[:nb] + b_ref[0]


def _modulation(c, w_ada, b_ada):
    nl, d, d6 = w_ada.shape
    nb = c.shape[0]
    return pl.pallas_call(
        _mod_kernel,
        grid=(nl, d6 // d),
        in_specs=[
            pl.BlockSpec((nb, d), lambda l, n: (0, 0)),
            pl.BlockSpec((1, d, d), lambda l, n: (l, 0, n)),
            pl.BlockSpec((1, 1, d), lambda l, n: (l, 0, n)),
        ],
        out_specs=pl.BlockSpec((1, nb, d), lambda l, n: (l, 0, n)),
        out_shape=jax.ShapeDtypeStruct((nl, nb, d6), F32),
        compiler_params=pltpu.CompilerParams(
            dimension_semantics=("arbitrary", "arbitrary"), vmem_limit_bytes=VMEM_LIMIT_BYTES),
        name="adaln_modulation",
    )(c, w_ada, b_ada.reshape(nl, 1, d6))


def _inproj_kernel(x_ref, halo_ref, mod_ref, w_ref, mu_ref, pool_ref, rkv_ref, lora_ref, gate_ref, *, tm):
    i = pl.program_id(1)
    scale = 1.0 + mod_ref[0, 0, 1:2, :]
    shift = mod_ref[0, 0, 0:1, :]
    h_cur = x_ref[0] * scale + shift
    h_halo = jnp.where(i == 0, 0.0, halo_ref[0] * scale + shift)
    h_ext = jnp.concatenate([h_halo, h_cur], axis=0).astype(BF16)
    h = h_cur.astype(BF16)
    nt = IN_COL_TILE

    def shifted(p, j0, j1):
        cur = p[SHIFT_HALO:]
        prev = pltpu.roll(p, 1, 0)[SHIFT_HALO:]
        return cur + (prev - cur) * mu_ref[0, :, j0:j1]

    def pool_out(p):
        pool_ref[0] = p.astype(BF16)

    def rkv_out(p, j):
        rkv_ref[0, :, j:j + nt] = shifted(p, j, j + nt).astype(BF16)

    def lora_out(p):
        xs = shifted(p, 3 * D_RWKV, D_SHIFT_PAD)
        lora_ref[0, :, LORA_W_OFF:LORA_A_OFF] = jnp.tanh(xs[:, LORA_W_OFF:LORA_A_OFF]).astype(BF16)
        lora_ref[0, :, LORA_A_OFF:LORA_G_OFF] = xs[:, LORA_A_OFF:LORA_G_OFF].astype(BF16)
        lora_ref[0, :, LORA_G_OFF:] = jax.nn.sigmoid(xs[:, LORA_G_OFF:]).astype(BF16)

    def gate_out(p, j):
        gate_ref[0, :, j:j + nt] = jax.nn.sigmoid(p).astype(BF16)

    tasks = [(h, 0, D_POOL, pool_out)]
    tasks += [(h_ext, D_POOL + j, nt, functools.partial(rkv_out, j=j)) for j in range(0, 3 * D_RWKV, nt)]
    tasks += [(h_ext, D_POOL + 3 * D_RWKV, D_LORA_PAD, lora_out)]
    off = D_POOL + D_SHIFT_PAD
    tasks += [(h, off + j, nt, functools.partial(gate_out, j=j)) for j in range(0, 2 * D_MODEL, nt)]
    pending = None
    for lhs, c0, width, epilogue in tasks:
        p = _dot(lhs, w_ref[0, :, c0:c0 + width])
        if pending is not None:
            pending[1](pending[0])
        pending = (p, epilogue)
    pending[1](pending[0])


def _inproj(x, mod, w_in_p, mu_p, l, tm):
    nb, s, d = x.shape
    hpb = tm // SHIFT_HALO
    return pl.pallas_call(
        functools.partial(_inproj_kernel, tm=tm),
        grid=(nb, s // tm),
        in_specs=[
            pl.BlockSpec((1, tm, d), lambda b, i: (b, i, 0)),
            pl.BlockSpec((1, SHIFT_HALO, d), lambda b, i: (b, jnp.maximum(i * hpb - 1, 0), 0)),
            pl.BlockSpec((1, 1, 6, d), lambda b, i: (l, b, 0, 0)),
            pl.BlockSpec((1, d, D_IN_PAD), lambda b, i: (l, 0, 0), pipeline_mode=pl.Buffered(1)),
            pl.BlockSpec((1, 1, D_SHIFT_PAD), lambda b, i: (l, 0, 0)),
        ],
        out_specs=[
            pl.BlockSpec((1, tm, D_POOL), lambda b, i: (b, i, 0)),
            pl.BlockSpec((1, tm, 3 * D_RWKV), lambda b, i: (b, i, 0)),
            pl.BlockSpec((1, tm, D_LORA_PAD), lambda b, i: (b, i, 0)),
            pl.BlockSpec((1, tm, 2 * D_MODEL), lambda b, i: (b, i, 0)),
        ],
        out_shape=[
            jax.ShapeDtypeStruct((nb, s, D_POOL), BF16),
            jax.ShapeDtypeStruct((nb, s, 3 * D_RWKV), BF16),
            jax.ShapeDtypeStruct((nb, s, D_LORA_PAD), BF16),
            jax.ShapeDtypeStruct((nb, s, 2 * D_MODEL), BF16),
        ],
        compiler_params=pltpu.CompilerParams(
            dimension_semantics=("parallel", "parallel"), vmem_limit_bytes=VMEM_LIMIT_BYTES),
        name="in_projection",
    )(x, x, mod, w_in_p, mu_p)


def _block_diag(y, bd_mask):
    return jnp.where(bd_mask, jnp.concatenate([y] * HEADS_PER_GROUP, axis=0), 0.0).astype(BF16)


def _mix_kernel(rkv_ref, lora_ref, wd_ref, wa_ref, wg_ref, vec_ref, o_ref, s_ref):
    c = pl.program_id(1)

    @pl.when(c == 0)
    def _():
        s_ref[...] = jnp.zeros(s_ref.shape, F32)

    row_g = lax.broadcasted_iota(jnp.int32, (GROUP, GROUP), 0)
    col_g = lax.broadcasted_iota(jnp.int32, (GROUP, GROUP), 1)
    bd_mask = (row_g // HEAD) == (col_g // HEAD)
    bd_ones = jnp.where(bd_mask, 1.0, 0.0).astype(BF16)
    row_c = lax.broadcasted_iota(jnp.int32, (CHUNK, GROUP), 0)
    col_c = lax.broadcasted_iota(jnp.int32, (CHUNK, GROUP), 1) % HEAD
    strict = col_c < row_c
    incl = col_c <= row_c
    eye_pk = jnp.where(col_c == row_c, 1.0, 0.0)
    tri2 = jnp.where(lax.broadcasted_iota(jnp.int32, (CHUNK, 2 * CHUNK), 1) % CHUNK
                     <= lax.broadcasted_iota(jnp.int32, (CHUNK, 2 * CHUNK), 0), 1.0, 0.0).astype(BF16)

    chains = [(b, g) for b in range(MIX_BATCH) for g in range(N_GROUPS)]
    n_ch = len(chains)
    G = range(n_ch)
    cols = [(g * GROUP, (g + 1) * GROUP) for _, g in chains]

    def vec(row, i):
        return vec_ref[0, row:row + 1, cols[i][0]:cols[i][1]]

    def rows(stacked, i):
        return stacked[i * CHUNK:(i + 1) * CHUNK]

    def seg_sum(parts):
        return _dot(jnp.concatenate([p.astype(BF16) for p in parts], axis=0), bd_ones)

    r = [rkv_ref[b, :, c0:c1].astype(F32) for (b, _), (c0, c1) in zip(chains, cols)]
    k = [rkv_ref[b, :, D_RWKV + c0:D_RWKV + c1].astype(F32) for (b, _), (c0, c1) in zip(chains, cols)]
    v = [rkv_ref[b, :, 2 * D_RWKV + c0:2 * D_RWKV + c1].astype(F32) for (b, _), (c0, c1) in zip(chains, cols)]

    tanh_xw = jnp.concatenate([lora_ref[b, :, LORA_W_OFF:LORA_A_OFF] for b in range(MIX_BATCH)], axis=0)
    xa = jnp.concatenate([lora_ref[b, :, LORA_A_OFF:LORA_G_OFF] for b in range(MIX_BATCH)], axis=0)
    sig_xg = jnp.concatenate([lora_ref[b, :, LORA_G_OFF:] for b in range(MIX_BATCH)], axis=0)
    z_g = [_dot(tanh_xw, wd_ref[0, :, g * GROUP:(g + 1) * GROUP]) for g in range(N_GROUPS)]
    a_g = [_dot(xa, wa_ref[0, :, g * GROUP:(g + 1) * GROUP]) for g in range(N_GROUPS)]
    gate_g = [_dot(sig_xg, wg_ref[0, :, g * GROUP:(g + 1) * GROUP]) for g in range(N_GROUPS)]
    ld = [-EXP_NEG_HALF * jax.nn.sigmoid(vec(0, i) + rows(z_g[g], b)) for i, (b, g) in enumerate(chains)]
    a = [jax.nn.sigmoid(vec(1, i) + rows(a_g[g], b)) for i, (b, g) in enumerate(chains)]
    gate = [rows(gate_g[g], b) for (b, g) in chains]

    kk = [k[i] * vec(2, i) for i in G]
    n2 = seg_sum([kk[i] * kk[i] for i in G])
    kkn = [kk[i] * jnp.minimum(lax.rsqrt(rows(n2, i)), 1e12) for i in G]
    k2 = [k[i] * (1.0 + (a[i] - 1.0) * vec(3, i)) for i in G]

    ld_hi = [ld[i].astype(BF16) for i in G]
    ld_lo = [(ld[i] - ld_hi[i].astype(F32)).astype(BF16) for i in G]
    cum = [_dot(tri2, jnp.concatenate([ld_hi[i], ld_lo[i]], axis=0)) for i in G]
    w_inc = [jnp.exp(cum[i]) for i in G]
    w_exc = [jnp.exp(cum[i] - ld[i]) for i in G]
    w_inv = [jnp.exp(-cum[i]) for i in G]

    rt = [(r[i] * w_inc[i]).astype(BF16) for i in G]
    at = [(-kkn[i] * w_exc[i]).astype(BF16) for i in G]
    bt_f = [kkn[i] * a[i] * w_inv[i] for i in G]
    kt_f = [k2[i] * w_inv[i] for i in G]
    lhs_ar = [jnp.concatenate([at[i], rt[i]], axis=0) for i in G]

    a1 = [_dot_nt(lhs_ar[i], _block_diag(bt_f[i], bd_mask)) for i in G]
    a2 = [_dot_nt(lhs_ar[i], _block_diag(kt_f[i], bd_mask)) for i in G]
    aab = [jnp.where(strict, a1[i][:CHUNK], 0.0) for i in G]
    arb = [jnp.where(incl, a1[i][CHUNK:], 0.0) for i in G]
    aak = [jnp.where(strict, a2[i][:CHUNK], 0.0) for i in G]
    ark = [jnp.where(incl, a2[i][CHUNK:], 0.0) for i in G]

    s = [s_ref[b, g] for b, g in chains]
    xr = [_dot_nt(lhs_ar[i], jnp.where(bd_mask, s[i], 0.0).astype(BF16)) for i in G]
    av = [_dot(jnp.concatenate([aak[i], ark[i]], axis=0).astype(BF16), _block_diag(v[i], bd_mask)) for i in G]
    x = [xr[i][:CHUNK] + av[i][:CHUNK] for i in G]

    t = [eye_pk + aab[i] for i in G]
    p = [_dot(aab[i].astype(BF16), _block_diag(aab[i], bd_mask)) for i in G]
    for _ in range(4):
        pt = [_dot(jnp.concatenate([p[i], t[i]], axis=0).astype(BF16), _block_diag(p[i], bd_mask)) for i in G]
        t = [t[i] + pt[i][CHUNK:] for i in G]
        p = [pt[i][:CHUNK] for i in G]
    t = [t[i] + _dot(t[i].astype(BF16), _block_diag(p[i], bd_mask)) for i in G]

    u = [_dot(t[i].astype(BF16), _block_diag(x[i], bd_mask)) for i in G]
    o = [xr[i][CHUNK:] + av[i][CHUNK:] + _dot(arb[i].astype(BF16), _block_diag(u[i], bd_mask)) for i in G]
    m = [_dot_tn(jnp.concatenate([u[i].astype(BF16), v[i].astype(BF16)], axis=0),
                 jnp.concatenate([bt_f[i].astype(BF16), kt_f[i].astype(BF16)], axis=0)) for i in G]
    for i, (b, g) in enumerate(chains):
        s_ref[b, g] = (s[i] + m[i]) * w_inc[i][CHUNK - 1:CHUNK, :]

    mean = seg_sum(o) * (1.0 / HEAD)
    d = [o[i] - rows(mean, i) for i in G]
    var = seg_sum([d[i] * d[i] for i in G]) * (1.0 / HEAD)
    rk = seg_sum([r[i] * k2[i] * vec(4, i) for i in G])
    for i, (b, g) in enumerate(chains):
        on = d[i] * lax.rsqrt(rows(var, i) + GN_EPS) * vec(5, i) + vec(6, i)
        o_ref[b, :, cols[i][0]:cols[i][1]] = ((on + rows(rk, i) * v[i]) * gate[i]).astype(BF16)


def _mixer(rkv, lora, wd, wa, wg, vecs, l):
    nb, s, _ = rkv.shape
    layer = lambda b, c: (l, 0, 0)
    return pl.pallas_call(
        _mix_kernel,
        grid=(nb // MIX_BATCH, s // CHUNK),
        in_specs=[
            pl.BlockSpec((MIX_BATCH, CHUNK, 3 * D_RWKV), lambda b, c: (b, c, 0)),
            pl.BlockSpec((MIX_BATCH, CHUNK, D_LORA_PAD), lambda b, c: (b, c, 0)),
            pl.BlockSpec((1, 128, D_RWKV), layer),
            pl.BlockSpec((1, 128, D_RWKV), layer),
            pl.BlockSpec((1, 256, D_RWKV), layer),
            pl.BlockSpec((1, 8, D_RWKV), layer),
        ],
        out_specs=pl.BlockSpec((MIX_BATCH, CHUNK, D_RWKV), lambda b, c: (b, c, 0)),
        out_shape=jax.ShapeDtypeStruct((nb, s, D_RWKV), BF16),
        scratch_shapes=[pltpu.VMEM((MIX_BATCH, N_GROUPS, GROUP, GROUP), F32)],
        compiler_params=pltpu.CompilerParams(
            dimension_semantics=("arbitrary", "arbitrary"), vmem_limit_bytes=VMEM_LIMIT_BYTES),
        name="rwkv7_mixer",
    )(rkv, lora, wd, wa, wg, vecs)


def _out_kernel(pool_ref, halo_ref, o_ref, gate_ref, x_ref, mod_ref, pmix_ref, pscale_ref, wpu_ref,
                wru_ref, wout_ref, lng_ref, lnb_ref, xo_ref, *, alpha, tm):
    i = pl.program_id(1)
    halo = jnp.where(i == 0, 0.0, halo_ref[0].astype(F32))
    ext = jnp.concatenate([halo, pool_ref[0].astype(F32)], axis=0)
    t_idx = i * tm + lax.broadcasted_iota(jnp.int32, (tm, POOL_GC), 0)
    ms = []
    for gi, w in enumerate(POOL_WINDOWS):
        e = ext[:, gi * POOL_GC:(gi + 1) * POOL_GC]
        acc, span = e, 1
        while span < w:
            acc = acc + pltpu.roll(acc, span, 0)
            span *= 2
        cnt = jnp.minimum(t_idx + 1, w).astype(F32)
        ms.append((acc[POOL_HALO:] / cnt - e[POOL_HALO:]).astype(BF16))

    rs = [slice(r * (tm // OUT_SUB), (r + 1) * (tm // OUT_SUB)) for r in range(OUT_SUB)]
    n_pw = len(POOL_WINDOWS)
    mixed_pool = [[_dot(ms[gi][r], pmix_ref[0, gi]) * pscale_ref[0, :, gi * POOL_GC:(gi + 1) * POOL_GC]
                   for gi in range(n_pw)] for r in rs]
    m_all = [jnp.concatenate(mp, axis=1).astype(BF16) for mp in mixed_pool]
    y_pool = [_dot(m, wpu_ref[0]) for m in m_all]
    y_rwkv = [_dot(o_ref[0, r, :], wru_ref[0]) for r in rs]
    y = [(gate_ref[0, r, 0:D_MODEL].astype(F32) * yp + gate_ref[0, r, D_MODEL:].astype(F32) * yr).astype(BF16)
         for r, yp, yr in zip(rs, y_pool, y_rwkv)]
    mixed = [_dot(yy, wout_ref[0]) for yy in y]
    for r, mx in zip(rs, mixed):
        z = alpha * x_ref[0, r, :] + (1.0 + mod_ref[0, 0, 2:3, :]) * mx
        xo_ref[0, r, :] = _layer_norm(z, lng_ref[0], lnb_ref[0], LN_EPS)


def _out_stage(pool, o, gates, x, mod, pmix, pscale, wpu, wru, wout, lng, lnb, l, tm, alpha):
    nb, s, d = x.shape
    layer = lambda b, i: (l, 0, 0)
    hpb = tm // POOL_HALO
    return pl.pallas_call(
        functools.partial(_out_kernel, alpha=alpha, tm=tm),
        grid=(nb, s // tm),
        in_specs=[
            pl.BlockSpec((1, tm, D_POOL), lambda b, i: (b, i, 0)),
            pl.BlockSpec((1, POOL_HALO, D_POOL), lambda b, i: (b, jnp.maximum(i * hpb - 1, 0), 0)),
            pl.BlockSpec((1, tm, D_RWKV), lambda b, i: (b, i, 0)),
            pl.BlockSpec((1, tm, 2 * D_MODEL), lambda b, i: (b, i, 0)),
            pl.BlockSpec((1, tm, d), lambda b, i: (b, i, 0)),
            pl.BlockSpec((1, 1, 6, d), lambda b, i: (l, b, 0, 0)),
            pl.BlockSpec((1, len(POOL_WINDOWS), POOL_GC, POOL_GC), lambda b, i: (l, 0, 0, 0)),
            pl.BlockSpec((1, 1, D_POOL), layer),
            pl.BlockSpec((1, D_POOL, d), layer),
            pl.BlockSpec((1, D_RWKV, d), layer),
            pl.BlockSpec((1, d, d), layer),
            pl.BlockSpec((1, 1, d), layer),
            pl.BlockSpec((1, 1, d), layer),
        ],
        out_specs=pl.BlockSpec((1, tm, d), lambda b, i: (b, i, 0)),
        out_shape=jax.ShapeDtypeStruct((nb, s, d), F32),
        compiler_params=pltpu.CompilerParams(
            dimension_semantics=("parallel", "parallel"), vmem_limit_bytes=VMEM_LIMIT_BYTES),
        name="merge_out_projection",
    )(pool, pool, o, gates, x, mod, pmix, pscale, wpu, wru, wout, lng, lnb)


def _ffn_kernel(x_ref, mod_ref, w1_ref, w2_ref, lng_ref, lnb_ref, xo_ref, hid_ref, *, alpha):
    x = x_ref[0]
    h = (x * (1.0 + mod_ref[0, 0, 4:5, :]) + mod_ref[0, 0, 3:4, :]).astype(BF16)
    nt = IN_COL_TILE
    for j in range(0, D_FF, nt):
        a = jnp.maximum(_dot(h, w1_ref[0, :, j:j + nt]), 0.0)
        hid_ref[:, j:j + nt] = (a * a).astype(BF16)
    f = _dot(hid_ref[...], w2_ref[0])
    z = alpha * x + (1.0 + mod_ref[0, 0, 5:6, :]) * f
    xo_ref[0] = _layer_norm(z, lng_ref[0], lnb_ref[0], LN_EPS)


def _ffn_stage(x, mod, w1, w2, lng, lnb, l, tm, alpha):
    nb, s, d = x.shape
    layer = lambda b, i: (l, 0, 0)
    return pl.pallas_call(
        functools.partial(_ffn_kernel, alpha=alpha),
        grid=(nb, s // tm),
        in_specs=[
            pl.BlockSpec((1, tm, d), lambda b, i: (b, i, 0)),
            pl.BlockSpec((1, 1, 6, d), lambda b, i: (l, b, 0, 0)),
            pl.BlockSpec((1, d, D_FF), layer, pipeline_mode=pl.Buffered(1)),
            pl.BlockSpec((1, D_FF, d), layer, pipeline_mode=pl.Buffered(1)),
            pl.BlockSpec((1, 1, d), layer),
            pl.BlockSpec((1, 1, d), layer),
        ],
        out_specs=pl.BlockSpec((1, tm, d), lambda b, i: (b, i, 0)),
        out_shape=jax.ShapeDtypeStruct((nb, s, d), F32),
        scratch_shapes=[pltpu.VMEM((tm, D_FF), BF16)],
        compiler_params=pltpu.CompilerParams(
            dimension_semantics=("parallel", "parallel"), vmem_limit_bytes=VMEM_LIMIT_BYTES),
        name="relu2_mlp",
    )(x, mod, w1, w2, lng, lnb)


def _pad_last(w, width):
    return jnp.pad(w, [(0, 0)] * (w.ndim - 1) + [(0, width - w.shape[-1])])


def _pad_rows(w, height):
    return jnp.pad(w, ((0, 0), (0, height - w.shape[1]), (0, 0)))


def _prep_in_weights(w_in, mu_shift):
    o_l = D_POOL + 3 * D_RWKV
    o_a = o_l + D_DECAY_LORA
    o_g = o_a + D_AAA_LORA
    o_gate = o_g + D_GATE_LORA
    w_p = jnp.concatenate([
        w_in[..., :o_l],
        _pad_last(w_in[..., o_l:o_a], LORA_A_OFF - LORA_W_OFF),
        _pad_last(w_in[..., o_a:o_g], LORA_G_OFF - LORA_A_OFF),
        _pad_last(w_in[..., o_g:o_gate], D_LORA_PAD - LORA_G_OFF),
        w_in[..., o_gate:],
    ], axis=-1).astype(BF16)
    m = mu_shift[:, None, :]
    r3 = 3 * D_RWKV
    mu_p = jnp.concatenate([
        m[..., :r3],
        _pad_last(m[..., r3:r3 + D_DECAY_LORA], LORA_A_OFF - LORA_W_OFF),
        _pad_last(m[..., r3 + D_DECAY_LORA:r3 + D_DECAY_LORA + D_AAA_LORA], LORA_G_OFF - LORA_A_OFF),
        _pad_last(m[..., r3 + D_DECAY_LORA + D_AAA_LORA:], D_LORA_PAD - LORA_G_OFF),
    ], axis=-1)
    return w_p, mu_p


def kernel(x, c, w_ada, b_ada, w_in, mu_shift, pool_mix, pool_scale, w_pool_up, w0, w_decay_up, a0,
           w_aaa_up, w_gate_up, k_k, k_a, r_k, gn_g, gn_b, w_rwkv_up, w_out, ln1_g, ln1_b, w_ff1, w_ff2,
           ln2_g, ln2_b):
    nb, s, d = x.shape
    depth = w_ada.shape[0]
    assert d == D_MODEL and s % CHUNK == 0 and nb % MIX_BATCH == 0
    alpha = (2 * depth) ** 0.25
    tm = min(ROW_TILE, s)
    assert s % tm == 0 and tm % POOL_HALO == 0

    mod = _modulation(c, w_ada, b_ada).reshape(depth, nb, 6, d)
    w_in_p, mu_p = _prep_in_weights(w_in, mu_shift)
    vecs = jnp.stack([w0, a0, k_k, k_a, r_k.reshape(depth, D_RWKV), gn_g, gn_b, jnp.zeros_like(w0)], axis=1)
    wd = _pad_rows(w_decay_up, LORA_A_OFF - LORA_W_OFF).astype(BF16)
    wa = _pad_rows(w_aaa_up, LORA_G_OFF - LORA_A_OFF).astype(BF16)
    wg = _pad_rows(w_gate_up, D_LORA_PAD - LORA_G_OFF).astype(BF16)
    pmix, wpu, wru, wout = (w.astype(BF16) for w in (pool_mix, w_pool_up, w_rwkv_up, w_out))
    w1, w2 = w_ff1.astype(BF16), w_ff2.astype(BF16)
    pscale, g1, b1, g2, b2 = (v[:, None, :] for v in (pool_scale, ln1_g, ln1_b, ln2_g, ln2_b))

    for l in range(depth):
        pool, rkv, lora, gates = _inproj(x, mod, w_in_p, mu_p, l, tm)
        o = _mixer(rkv, lora, wd, wa, wg, vecs, l)
        x = _out_stage(pool, o, gates, x, mod, pmix, pscale, wpu, wru, wout, g1, b1, l, tm, alpha)
        x = _ffn_stage(x, mod, w1, w2, g2, b2, l, tm, alpha)
    return x
```

```python
import functools
import math

import jax
import jax.numpy as jnp
from jax import lax
from jax.experimental import pallas as pl
from jax.experimental.pallas import tpu as pltpu

F32 = jnp.float32
BF16 = jnp.bfloat16

D_MODEL = 1024
POOL_WINDOWS = (2, 4, 8, 16)
POOL_GC = 128
D_POOL = 512
HEAD = 64
D_RWKV = 1024
D_DECAY_LORA = 64
D_AAA_LORA = 64
D_GATE_LORA = 160
D_FF = 4096
LN_EPS = 1e-5
GN_EPS = 64e-5
EXP_NEG_HALF = math.exp(-0.5)
LOG2_DECAY_SCALE = -EXP_NEG_HALF * math.log2(math.e)

CHUNK = 64
GROUP = 256
HEADS_PER_GROUP = GROUP // HEAD
N_GROUPS = D_RWKV // GROUP
POOL_HALO = 16
SHIFT_HALO = 8
ROW_TILE = 1024
MIX_BATCH = 8
IN_COL_TILE = 512
OUT_SUB = 4
LORA_W_OFF, LORA_A_OFF, LORA_G_OFF, D_LORA_PAD = 0, 128, 256, 512
D_SHIFT_PAD = 3 * D_RWKV + D_LORA_PAD
VMEM_LIMIT_BYTES = 56 * 1024 * 1024


def _dot(a, b):
    return jnp.dot(a, b, preferred_element_type=F32)


def _dot_nt(a, b):
    return lax.dot_general(a, b, (((1,), (1,)), ((), ())), preferred_element_type=F32)


def _dot_tn(a, b):
    return lax.dot_general(a, b, (((0,), (0,)), ((), ())), preferred_element_type=F32)


def _layer_norm(z, g, b, eps):
    mu = jnp.mean(z, axis=-1, keepdims=True)
    d = z - mu
    var = jnp.mean(d * d, axis=-1, keepdims=True)
    return d * lax.rsqrt(var + eps) * g + b


def _mod_kernel(c_ref, w_ref, b_ref, o_ref):
    c = c_ref[...]
    cond = c * jax.nn.sigmoid(c)
    nb = cond.shape[0]
    lhs = jnp.concatenate([cond, jnp.zeros_like(cond)], axis=0).astype(BF16)
    res = _dot(lhs, w_ref[0].astype(BF16))
    o_ref[0] = res[:nb] + b_ref[0]


def _modulation(c, w_ada, b_ada):
    nl, d, d6 = w_ada.shape
    nb = c.shape[0]
    return pl.pallas_call(
        _mod_kernel,
        grid=(nl, d6 // d),
        in_specs=[
            pl.BlockSpec((nb, d), lambda l, n: (0, 0)),
            pl.BlockSpec((1, d, d), lambda l, n: (l, 0, n)),
            pl.BlockSpec((1, 1, d), lambda l, n: (l, 0, n)),
        ],
        out_specs=pl.BlockSpec((1, nb, d), lambda l, n: (l, 0, n)),
        out_shape=jax.ShapeDtypeStruct((nl, nb, d6), F32),
        compiler_params=pltpu.CompilerParams(
            dimension_semantics=("arbitrary", "arbitrary"), vmem_limit_bytes=VMEM_LIMIT_BYTES),
        name="adaln_modulation",
    )(c, w_ada, b_ada.reshape(nl, 1, d6))


def _inproj_kernel(x_ref, halo_ref, mod_ref, wm_ref, wl_ref, wg_ref, mu_ref,
                   pool_ref, rkv_ref, lora_ref, gate_ref, *, tm):
    i = pl.program_id(1)
    scale = 1.0 + mod_ref[0, 0, 1:2, :]
    shift = mod_ref[0, 0, 0:1, :]
    h_cur = x_ref[0] * scale + shift
    h_halo = jnp.where(i == 0, 0.0, halo_ref[0] * scale + shift)
    h_ext = jnp.concatenate([h_halo, h_cur], axis=0).astype(BF16)
    h = h_cur.astype(BF16)
    nt = IN_COL_TILE

    def shifted(p, j0, j1):
        cur = p[SHIFT_HALO:]
        prev = pltpu.roll(p, 1, 0)[SHIFT_HALO:]
        return cur + (prev - cur) * mu_ref[0, :, j0:j1]

    def pool_out(p):
        pool_ref[0] = p.astype(BF16)

    def rkv_out(p, j):
        rkv_ref[0, :, j:j + nt] = shifted(p, j, j + nt).astype(BF16)

    def lora_out(p):
        xs = shifted(p, 3 * D_RWKV, D_SHIFT_PAD)
        lora_ref[0, :, LORA_W_OFF:LORA_A_OFF] = jnp.tanh(xs[:, LORA_W_OFF:LORA_A_OFF]).astype(BF16)
        lora_ref[0, :, LORA_A_OFF:LORA_G_OFF] = xs[:, LORA_A_OFF:LORA_G_OFF].astype(BF16)
        lora_ref[0, :, LORA_G_OFF:] = jax.nn.sigmoid(xs[:, LORA_G_OFF:]).astype(BF16)

    def gate_out(p, j):
        gate_ref[0, :, j:j + nt] = jax.nn.sigmoid(p).astype(BF16)

    tasks = [(h, wm_ref, 0, D_POOL, pool_out)]
    tasks += [(h_ext, wm_ref, D_POOL + j, nt, functools.partial(rkv_out, j=j)) for j in range(0, 3 * D_RWKV, nt)]
    tasks += [(h_ext, wl_ref, 0, D_LORA_PAD, lora_out)]
    tasks += [(h, wg_ref, j, nt, functools.partial(gate_out, j=j)) for j in range(0, 2 * D_MODEL, nt)]
    pending = None
    for lhs, w_ref, c0, width, epilogue in tasks:
        p = _dot(lhs, w_ref[0, :, c0:c0 + width])
        if pending is not None:
            pending[1](pending[0])
        pending = (p, epilogue)
    pending[1](pending[0])


def _inproj(x, mod, w_main, w_lora, w_gate, mu_p, l, tm):
    nb, s, d = x.shape
    hpb = tm // SHIFT_HALO
    layer_w = lambda width: pl.BlockSpec((1, d, width), lambda b, i: (l, 0, 0), pipeline_mode=pl.Buffered(1))
    return pl.pallas_call(
        functools.partial(_inproj_kernel, tm=tm),
        grid=(nb, s // tm),
        in_specs=[
            pl.BlockSpec((1, tm, d), lambda b, i: (b, i, 0)),
            pl.BlockSpec((1, SHIFT_HALO, d), lambda b, i: (b, jnp.maximum(i * hpb - 1, 0), 0)),
            pl.BlockSpec((1, 1, 6, d), lambda b, i: (l, b, 0, 0)),
            layer_w(D_POOL + 3 * D_RWKV), layer_w(D_LORA_PAD), layer_w(2 * D_MODEL),
            pl.BlockSpec((1, 1, D_SHIFT_PAD), lambda b, i: (l, 0, 0)),
        ],
        out_specs=[
            pl.BlockSpec((1, tm, D_POOL), lambda b, i: (b, i, 0)),
            pl.BlockSpec((1, tm, 3 * D_RWKV), lambda b, i: (b, i, 0)),
            pl.BlockSpec((1, tm, D_LORA_PAD), lambda b, i: (b, i, 0)),
            pl.BlockSpec((1, tm, 2 * D_MODEL), lambda b, i: (b, i, 0)),
        ],
        out_shape=[
            jax.ShapeDtypeStruct((nb, s, D_POOL), BF16),
            jax.ShapeDtypeStruct((nb, s, 3 * D_RWKV), BF16),
            jax.ShapeDtypeStruct((nb, s, D_LORA_PAD), BF16),
            jax.ShapeDtypeStruct((nb, s, 2 * D_MODEL), BF16),
        ],
        compiler_params=pltpu.CompilerParams(
            dimension_semantics=("parallel", "parallel"), vmem_limit_bytes=VMEM_LIMIT_BYTES),
        name="in_projection",
    )(x, x, mod, w_main, w_lora, w_gate, mu_p)


def _block_diag(y, bd_mask):
    return jnp.where(bd_mask, jnp.concatenate([y] * HEADS_PER_GROUP, axis=0), 0.0).astype(BF16)


def _mix_kernel(rkv_ref, lora_ref, wd_ref, wa_ref, wg_ref, vec_ref, o_ref, s_ref):
    c = pl.program_id(1)

    @pl.when(c == 0)
    def _():
        s_ref[...] = jnp.zeros(s_ref.shape, F32)

    row_g = lax.broadcasted_iota(jnp.int32, (GROUP, GROUP), 0)
    col_g = lax.broadcasted_iota(jnp.int32, (GROUP, GROUP), 1)
    bd_mask = (row_g // HEAD) == (col_g // HEAD)
    bd_ones = jnp.where(bd_mask, 1.0, 0.0).astype(BF16)
    row_c = lax.broadcasted_iota(jnp.int32, (CHUNK, GROUP), 0)
    col_c = lax.broadcasted_iota(jnp.int32, (CHUNK, GROUP), 1) % HEAD
    strict = col_c < row_c
    incl = col_c <= row_c
    eye_pk = jnp.where(col_c == row_c, 1.0, 0.0)
    tri2 = jnp.where(lax.broadcasted_iota(jnp.int32, (CHUNK, 2 * CHUNK), 1) % CHUNK
                     <= lax.broadcasted_iota(jnp.int32, (CHUNK, 2 * CHUNK), 0), 1.0, 0.0).astype(BF16)

    chains = [(b, g) for b in range(MIX_BATCH) for g in range(N_GROUPS)]
    n_ch = len(chains)
    G = range(n_ch)
    cols = [(g * GROUP, (g + 1) * GROUP) for _, g in chains]

    def vec(row, i):
        return vec_ref[0, row:row + 1, cols[i][0]:cols[i][1]]

    def rows(stacked, i):
        return stacked[i * CHUNK:(i + 1) * CHUNK]

    def seg_sum(parts):
        return _dot(jnp.concatenate([p.astype(BF16) for p in parts], axis=0), bd_ones)

    r = [rkv_ref[b, :, c0:c1].astype(F32) for (b, _), (c0, c1) in zip(chains, cols)]
    k = [rkv_ref[b, :, D_RWKV + c0:D_RWKV + c1].astype(F32) for (b, _), (c0, c1) in zip(chains, cols)]
    v = [rkv_ref[b, :, 2 * D_RWKV + c0:2 * D_RWKV + c1].astype(F32) for (b, _), (c0, c1) in zip(chains, cols)]

    tanh_xw = jnp.concatenate([lora_ref[b, :, LORA_W_OFF:LORA_A_OFF] for b in range(MIX_BATCH)], axis=0)
    xa = jnp.concatenate([lora_ref[b, :, LORA_A_OFF:LORA_G_OFF] for b in range(MIX_BATCH)], axis=0)
    sig_xg = jnp.concatenate([lora_ref[b, :, LORA_G_OFF:] for b in range(MIX_BATCH)], axis=0)
    z_g = [_dot(tanh_xw, wd_ref[0, :, g * GROUP:(g + 1) * GROUP]) for g in range(N_GROUPS)]
    a_g = [_dot(xa, wa_ref[0, :, g * GROUP:(g + 1) * GROUP]) for g in range(N_GROUPS)]
    gate_g = [_dot(sig_xg, wg_ref[0, :, g * GROUP:(g + 1) * GROUP]) for g in range(N_GROUPS)]
    ld = [LOG2_DECAY_SCALE * jax.nn.sigmoid(vec(0, i) + rows(z_g[g], b)) for i, (b, g) in enumerate(chains)]
    a = [jax.nn.sigmoid(vec(1, i) + rows(a_g[g], b)) for i, (b, g) in enumerate(chains)]
    gate = [rows(gate_g[g], b) for (b, g) in chains]

    kk = [k[i] * vec(2, i) for i in G]
    n2 = seg_sum([kk[i] * kk[i] for i in G])
    kkn = [kk[i] * jnp.minimum(lax.rsqrt(rows(n2, i)), 1e12) for i in G]
    k2 = [k[i] * (1.0 + (a[i] - 1.0) * vec(3, i)) for i in G]

    ld_hi = [ld[i].astype(BF16) for i in G]
    ld_lo = [(ld[i] - ld_hi[i].astype(F32)).astype(BF16) for i in G]
    cum = [_dot(tri2, jnp.concatenate([ld_hi[i], ld_lo[i]], axis=0)) for i in G]
    w_inc = [jnp.exp2(cum[i]) for i in G]
    w_exc = [jnp.exp2(cum[i] - ld[i]) for i in G]
    w_inv = [jnp.exp2(-cum[i]) for i in G]

    rt = [(r[i] * w_inc[i]).astype(BF16) for i in G]
    at = [(-kkn[i] * w_exc[i]).astype(BF16) for i in G]
    bt_f = [kkn[i] * a[i] * w_inv[i] for i in G]
    kt_f = [k2[i] * w_inv[i] for i in G]
    lhs_ar = [jnp.concatenate([at[i], rt[i]], axis=0) for i in G]

    a1 = [_dot_nt(lhs_ar[i], _block_diag(bt_f[i], bd_mask)) for i in G]
    a2 = [_dot_nt(lhs_ar[i], _block_diag(kt_f[i], bd_mask)) for i in G]
    aab = [jnp.where(strict, a1[i][:CHUNK], 0.0) for i in G]
    arb = [jnp.where(incl, a1[i][CHUNK:], 0.0) for i in G]
    aak = [jnp.where(strict, a2[i][:CHUNK], 0.0) for i in G]
    ark = [jnp.where(incl, a2[i][CHUNK:], 0.0) for i in G]

    s = [s_ref[b, g] for b, g in chains]
    xr = [_dot_nt(lhs_ar[i], jnp.where(bd_mask, s[i], 0.0).astype(BF16)) for i in G]
    av = [_dot(jnp.concatenate([aak[i], ark[i]], axis=0).astype(BF16), _block_diag(v[i], bd_mask)) for i in G]
    x = [xr[i][:CHUNK] + av[i][:CHUNK] for i in G]

    t = [eye_pk + aab[i] for i in G]
    p = [_dot(aab[i].astype(BF16), _block_diag(aab[i], bd_mask)) for i in G]
    for _ in range(4):
        pt = [_dot(jnp.concatenate([p[i], t[i]], axis=0).astype(BF16), _block_diag(p[i], bd_mask)) for i in G]
        t = [t[i] + pt[i][CHUNK:] for i in G]
        p = [pt[i][:CHUNK] for i in G]
    t = [t[i] + _dot(t[i].astype(BF16), _block_diag(p[i], bd_mask)) for i in G]

    u = [_dot(t[i].astype(BF16), _block_diag(x[i], bd_mask)) for i in G]
    o = [xr[i][CHUNK:] + av[i][CHUNK:] + _dot(arb[i].astype(BF16), _block_diag(u[i], bd_mask)) for i in G]
    m = [_dot_tn(jnp.concatenate([u[i].astype(BF16), v[i].astype(BF16)], axis=0),
                 jnp.concatenate([bt_f[i].astype(BF16), kt_f[i].astype(BF16)], axis=0)) for i in G]
    for i, (b, g) in enumerate(chains):
        s_ref[b, g] = (s[i] + m[i]) * w_inc[i][CHUNK - 1:CHUNK, :]

    mean = seg_sum(o) * (1.0 / HEAD)
    d = [o[i] - rows(mean, i) for i in G]
    var = seg_sum([d[i] * d[i] for i in G]) * (1.0 / HEAD)
    rk = seg_sum([r[i] * k2[i] * vec(4, i) for i in G])
    for i, (b, g) in enumerate(chains):
        on = d[i] * lax.rsqrt(rows(var, i) + GN_EPS) * vec(5, i) + vec(6, i)
        o_ref[b, :, cols[i][0]:cols[i][1]] = ((on + rows(rk, i) * v[i]) * gate[i]).astype(BF16)


def _mixer(rkv, lora, wd, wa, wg, vecs, l):
    nb, s, _ = rkv.shape
    layer = lambda b, c: (l, 0, 0)
    return pl.pallas_call(
        _mix_kernel,
        grid=(nb // MIX_BATCH, s // CHUNK),
        in_specs=[
            pl.BlockSpec((MIX_BATCH, CHUNK, 3 * D_RWKV), lambda b, c: (b, c, 0)),
            pl.BlockSpec((MIX_BATCH, CHUNK, D_LORA_PAD), lambda b, c: (b, c, 0)),
            pl.BlockSpec((1, 128, D_RWKV), layer),
            pl.BlockSpec((1, 128, D_RWKV), layer),
            pl.BlockSpec((1, 256, D_RWKV), layer),
            pl.BlockSpec((1, 8, D_RWKV), layer),
        ],
        out_specs=pl.BlockSpec((MIX_BATCH, CHUNK, D_RWKV), lambda b, c: (b, c, 0)),
        out_shape=jax.ShapeDtypeStruct((nb, s, D_RWKV), BF16),
        scratch_shapes=[pltpu.VMEM((MIX_BATCH, N_GROUPS, GROUP, GROUP), F32)],
        compiler_params=pltpu.CompilerParams(
            dimension_semantics=("arbitrary", "arbitrary"), vmem_limit_bytes=VMEM_LIMIT_BYTES),
        name="rwkv7_mixer",
    )(rkv, lora, wd, wa, wg, vecs)


def _out_kernel(pool_ref, halo_ref, o_ref, gate_ref, x_ref, mod_ref, pmix_ref, pscale_ref, wpu_ref,
                wru_ref, wout_ref, lng_ref, lnb_ref, xo_ref, *, alpha, tm):
    i = pl.program_id(1)
    halo = jnp.where(i == 0, 0.0, halo_ref[0].astype(F32))
    ext = jnp.concatenate([halo, pool_ref[0].astype(F32)], axis=0)
    t_idx = i * tm + lax.broadcasted_iota(jnp.int32, (tm, POOL_GC), 0)
    ms = []
    for gi, w in enumerate(POOL_WINDOWS):
        e = ext[:, gi * POOL_GC:(gi + 1) * POOL_GC]
        acc, span = e, 1
        while span < w:
            acc = acc + pltpu.roll(acc, span, 0)
            span *= 2
        cnt = jnp.minimum(t_idx + 1, w).astype(F32)
        ms.append((acc[POOL_HALO:] / cnt - e[POOL_HALO:]).astype(BF16))

    rs = [slice(r * (tm // OUT_SUB), (r + 1) * (tm // OUT_SUB)) for r in range(OUT_SUB)]
    n_pw = len(POOL_WINDOWS)
    mixed_pool = [[_dot(ms[gi][r], pmix_ref[0, gi]) * pscale_ref[0, :, gi * POOL_GC:(gi + 1) * POOL_GC]
                   for gi in range(n_pw)] for r in rs]
    m_all = [jnp.concatenate(mp, axis=1).astype(BF16) for mp in mixed_pool]
    y_pool = [_dot(m, wpu_ref[0]) for m in m_all]
    y_rwkv = [_dot(o_ref[0, r, :], wru_ref[0]) for r in rs]
    y = [(gate_ref[0, r, 0:D_MODEL].astype(F32) * yp + gate_ref[0, r, D_MODEL:].astype(F32) * yr).astype(BF16)
         for r, yp, yr in zip(rs, y_pool, y_rwkv)]
    mixed = [_dot(yy, wout_ref[0]) for yy in y]
    for r, mx in zip(rs, mixed):
        z = alpha * x_ref[0, r, :] + (1.0 + mod_ref[0, 0, 2:3, :]) * mx
        xo_ref[0, r, :] = _layer_norm(z, lng_ref[0], lnb_ref[0], LN_EPS)


def _out_stage(pool, o, gates, x, mod, pmix, pscale, wpu, wru, wout, lng, lnb, l, tm, alpha):
    nb, s, d = x.shape
    layer = lambda b, i: (l, 0, 0)
    hpb = tm // POOL_HALO
    return pl.pallas_call(
        functools.partial(_out_kernel, alpha=alpha, tm=tm),
        grid=(nb, s // tm),
        in_specs=[
            pl.BlockSpec((1, tm, D_POOL), lambda b, i: (b, i, 0)),
            pl.BlockSpec((1, POOL_HALO, D_POOL), lambda b, i: (b, jnp.maximum(i * hpb - 1, 0), 0)),
            pl.BlockSpec((1, tm, D_RWKV), lambda b, i: (b, i, 0)),
            pl.BlockSpec((1, tm, 2 * D_MODEL), lambda b, i: (b, i, 0)),
            pl.BlockSpec((1, tm, d), lambda b, i: (b, i, 0)),
            pl.BlockSpec((1, 1, 6, d), lambda b, i: (l, b, 0, 0)),
            pl.BlockSpec((1, len(POOL_WINDOWS), POOL_GC, POOL_GC), lambda b, i: (l, 0, 0, 0)),
            pl.BlockSpec((1, 1, D_POOL), layer),
            pl.BlockSpec((1, D_POOL, d), layer),
            pl.BlockSpec((1, D_RWKV, d), layer),
            pl.BlockSpec((1, d, d), layer),
            pl.BlockSpec((1, 1, d), layer),
            pl.BlockSpec((1, 1, d), layer),
        ],
        out_specs=pl.BlockSpec((1, tm, d), lambda b, i: (b, i, 0)),
        out_shape=jax.ShapeDtypeStruct((nb, s, d), F32),
        compiler_params=pltpu.CompilerParams(
            dimension_semantics=("parallel", "parallel"), vmem_limit_bytes=VMEM_LIMIT_BYTES),
        name="merge_out_projection",
    )(pool, pool, o, gates, x, mod, pmix, pscale, wpu, wru, wout, lng, lnb)


def _ffn_kernel(x_ref, mod_ref, w1_ref, w2_ref, lng_ref, lnb_ref, xo_ref, hid_ref, *, alpha):
    x = x_ref[0]
    h = (x * (1.0 + mod_ref[0, 0, 4:5, :]) + mod_ref[0, 0, 3:4, :]).astype(BF16)
    nt = IN_COL_TILE
    for j in range(0, D_FF, nt):
        a = jnp.maximum(_dot(h, w1_ref[0, :, j:j + nt]), 0.0)
        hid_ref[:, j:j + nt] = (a * a).astype(BF16)
    f = _dot(hid_ref[...], w2_ref[0])
    z = alpha * x + (1.0 + mod_ref[0, 0, 5:6, :]) * f
    xo_ref[0] = _layer_norm(z, lng_ref[0], lnb_ref[0], LN_EPS)


def _ffn_stage(x, mod, w1, w2, lng, lnb, l, tm, alpha):
    nb, s, d = x.shape
    layer = lambda b, i: (l, 0, 0)
    return pl.pallas_call(
        functools.partial(_ffn_kernel, alpha=alpha),
        grid=(nb, s // tm),
        in_specs=[
            pl.BlockSpec((1, tm, d), lambda b, i: (b, i, 0)),
            pl.BlockSpec((1, 1, 6, d), lambda b, i: (l, b, 0, 0)),
            pl.BlockSpec((1, d, D_FF), layer, pipeline_mode=pl.Buffered(1)),
            pl.BlockSpec((1, D_FF, d), layer, pipeline_mode=pl.Buffered(1)),
            pl.BlockSpec((1, 1, d), layer),
            pl.BlockSpec((1, 1, d), layer),
        ],
        out_specs=pl.BlockSpec((1, tm, d), lambda b, i: (b, i, 0)),
        out_shape=jax.ShapeDtypeStruct((nb, s, d), F32),
        scratch_shapes=[pltpu.VMEM((tm, D_FF), BF16)],
        compiler_params=pltpu.CompilerParams(
            dimension_semantics=("parallel", "parallel"), vmem_limit_bytes=VMEM_LIMIT_BYTES),
        name="relu2_mlp",
    )(x, mod, w1, w2, lng, lnb)


def _pad_last(w, width):
    return jnp.pad(w, [(0, 0)] * (w.ndim - 1) + [(0, width - w.shape[-1])])


def _pad_rows(w, height):
    return jnp.pad(w, ((0, 0), (0, height - w.shape[1]), (0, 0)))


def _prep_in_weights(w_in, mu_shift):
    o_l = D_POOL + 3 * D_RWKV
    o_a = o_l + D_DECAY_LORA
    o_g = o_a + D_AAA_LORA
    o_gate = o_g + D_GATE_LORA

    def lora_pad(t, base):
        return jnp.concatenate([
            _pad_last(t[..., base:base + D_DECAY_LORA], LORA_A_OFF - LORA_W_OFF),
            _pad_last(t[..., base + D_DECAY_LORA:base + D_DECAY_LORA + D_AAA_LORA], LORA_G_OFF - LORA_A_OFF),
            _pad_last(t[..., base + D_DECAY_LORA + D_AAA_LORA:base + o_gate - o_l], D_LORA_PAD - LORA_G_OFF),
        ], axis=-1)

    w_main = w_in[..., :o_l].astype(BF16)
    w_lora = lora_pad(w_in, o_l).astype(BF16)
    w_gate = w_in[..., o_gate:].astype(BF16)
    m = mu_shift[:, None, :]
    mu_p = jnp.concatenate([m[..., :3 * D_RWKV], lora_pad(m, 3 * D_RWKV)], axis=-1)
    return w_main, w_lora, w_gate, mu_p


def kernel(x, c, w_ada, b_ada, w_in, mu_shift, pool_mix, pool_scale, w_pool_up, w0, w_decay_up, a0,
           w_aaa_up, w_gate_up, k_k, k_a, r_k, gn_g, gn_b, w_rwkv_up, w_out, ln1_g, ln1_b, w_ff1, w_ff2,
           ln2_g, ln2_b):
    nb, s, d = x.shape
    depth = w_ada.shape[0]
    assert d == D_MODEL and s % CHUNK == 0 and nb % MIX_BATCH == 0
    alpha = (2 * depth) ** 0.25
    tm = min(ROW_TILE, s)
    assert s % tm == 0 and tm % POOL_HALO == 0

    mod = _modulation(c, w_ada, b_ada).reshape(depth, nb, 6, d)
    w_main, w_lora, w_gate, mu_p = _prep_in_weights(w_in, mu_shift)
    vecs = jnp.stack([w0, a0, k_k, k_a, r_k.reshape(depth, D_RWKV), gn_g, gn_b, jnp.zeros_like(w0)], axis=1)
    wd = _pad_rows(w_decay_up, LORA_A_OFF - LORA_W_OFF).astype(BF16)
    wa = _pad_rows(w_aaa_up, LORA_G_OFF - LORA_A_OFF).astype(BF16)
    wg = _pad_rows(w_gate_up, D_LORA_PAD - LORA_G_OFF).astype(BF16)
    pmix, wpu, wru, wout = (w.astype(BF16) for w in (pool_mix, w_pool_up, w_rwkv_up, w_out))
    w1, w2 = w_ff1.astype(BF16), w_ff2.astype(BF16)
    pscale, g1, b1, g2, b2 = (v[:, None, :] for v in (pool_scale, ln1_g, ln1_b, ln2_g, ln2_b))

    for l in range(depth):
        pool, rkv, lora, gates = _inproj(x, mod, w_main, w_lora, w_gate, mu_p, l, tm)
        o = _mixer(rkv, lora, wd, wa, wg, vecs, l)
        x = _out_stage(pool, o, gates, x, mod, pmix, pscale, wpu, wru, wout, g1, b1, l, tm, alpha)
        x = _ffn_stage(x, mod, w1, w2, g2, b2, l, tm, alpha)
    return x
```

```python
import functools
import math

import jax
import jax.numpy as jnp
from jax import lax
from jax.experimental import pallas as pl
from jax.experimental.pallas import tpu as pltpu

F32 = jnp.float32
BF16 = jnp.bfloat16

D_MODEL = 1024
POOL_WINDOWS = (2, 4, 8, 16)
POOL_GC = 128
D_POOL = 512
HEAD = 64
D_RWKV = 1024
D_DECAY_LORA = 64
D_AAA_LORA = 64
D_GATE_LORA = 160
D_FF = 4096
LN_EPS = 1e-5
GN_EPS = 64e-5
EXP_NEG_HALF = math.exp(-0.5)
LOG2_DECAY_SCALE = -EXP_NEG_HALF * math.log2(math.e)

CHUNK = 64
GROUP = 256
HEADS_PER_GROUP = GROUP // HEAD
N_GROUPS = D_RWKV // GROUP
POOL_HALO = 16
SHIFT_HALO = 8
ROW_TILE = 1024
MIX_BATCH = 8
IN_COL_TILE = 512
OUT_SUB = 4
LORA_W_OFF, LORA_A_OFF, LORA_G_OFF, D_LORA_PAD = 0, 128, 256, 512
D_SHIFT_PAD = 3 * D_RWKV + D_LORA_PAD
VMEM_LIMIT_BYTES = 56 * 1024 * 1024


def _dot(a, b):
    return jnp.dot(a, b, preferred_element_type=F32)


def _dot_nt(a, b):
    return lax.dot_general(a, b, (((1,), (1,)), ((), ())), preferred_element_type=F32)


def _dot_tn(a, b):
    return lax.dot_general(a, b, (((0,), (0,)), ((), ())), preferred_element_type=F32)


def _layer_norm(z, g, b, eps):
    mu = jnp.mean(z, axis=-1, keepdims=True)
    d = z - mu
    var = jnp.mean(d * d, axis=-1, keepdims=True)
    return d * lax.rsqrt(var + eps) * g + b


def _mod_kernel(c_ref, w_ref, b_ref, o_ref):
    c = c_ref[...]
    cond = c * jax.nn.sigmoid(c)
    nb = cond.shape[0]
    lhs = jnp.concatenate([cond, jnp.zeros_like(cond)], axis=0).astype(BF16)
    res = _dot(lhs, w_ref[0].astype(BF16))
    o_ref[0] = res[:nb] + b_ref[0]


def _modulation(c, w_ada, b_ada):
    nl, d, d6 = w_ada.shape
    nb = c.shape[0]
    return pl.pallas_call(
        _mod_kernel,
        grid=(nl, d6 // d),
        in_specs=[
            pl.BlockSpec((nb, d), lambda l, n: (0, 0)),
            pl.BlockSpec((1, d, d), lambda l, n: (l, 0, n)),
            pl.BlockSpec((1, 1, d), lambda l, n: (l, 0, n)),
        ],
        out_specs=pl.BlockSpec((1, nb, d), lambda l, n: (l, 0, n)),
        out_shape=jax.ShapeDtypeStruct((nl, nb, d6), F32),
        compiler_params=pltpu.CompilerParams(
            dimension_semantics=("arbitrary", "arbitrary"), vmem_limit_bytes=VMEM_LIMIT_BYTES),
        name="adaln_modulation",
    )(c, w_ada, b_ada.reshape(nl, 1, d6))


def _inproj_kernel(x_ref, halo_ref, mod_ref, wm_ref, wl_ref, wg_ref, mu_ref,
                   pool_ref, rkv_ref, lora_ref, gate_ref, *, tm):
    i = pl.program_id(1)
    scale = 1.0 + mod_ref[0, 0, 1:2, :]
    shift = mod_ref[0, 0, 0:1, :]
    h_cur = x_ref[0] * scale + shift
    h_halo = jnp.where(i == 0, 0.0, halo_ref[0] * scale + shift)
    h_ext = jnp.concatenate([h_halo, h_cur], axis=0).astype(BF16)
    h = h_cur.astype(BF16)
    nt = IN_COL_TILE

    def shifted(p, j0, j1):
        cur = p[SHIFT_HALO:]
        prev = pltpu.roll(p, 1, 0)[SHIFT_HALO:]
        return cur + (prev - cur) * mu_ref[0, :, j0:j1]

    def pool_out(p):
        pool_ref[0] = p.astype(BF16)

    def rkv_out(p, j):
        rkv_ref[0, :, j:j + nt] = shifted(p, j, j + nt).astype(BF16)

    def lora_out(p):
        xs = shifted(p, 3 * D_RWKV, D_SHIFT_PAD)
        lora_ref[0, :, LORA_W_OFF:LORA_A_OFF] = jnp.tanh(xs[:, LORA_W_OFF:LORA_A_OFF]).astype(BF16)
        lora_ref[0, :, LORA_A_OFF:LORA_G_OFF] = xs[:, LORA_A_OFF:LORA_G_OFF].astype(BF16)
        lora_ref[0, :, LORA_G_OFF:] = jax.nn.sigmoid(xs[:, LORA_G_OFF:]).astype(BF16)

    def gate_out(p, j):
        gate_ref[0, :, j:j + nt] = jax.nn.sigmoid(p).astype(BF16)

    tasks = [(h, wm_ref, 0, D_POOL, pool_out)]
    tasks += [(h_ext, wm_ref, D_POOL + j, nt, functools.partial(rkv_out, j=j)) for j in range(0, 3 * D_RWKV, nt)]
    tasks += [(h_ext, wl_ref, 0, D_LORA_PAD, lora_out)]
    tasks += [(h, wg_ref, j, nt, functools.partial(gate_out, j=j)) for j in range(0, 2 * D_MODEL, nt)]
    pending = None
    for lhs, w_ref, c0, width, epilogue in tasks:
        p = _dot(lhs, w_ref[0, :, c0:c0 + width])
        if pending is not None:
            pending[1](pending[0])
        pending = (p, epilogue)
    pending[1](pending[0])


def _inproj(x, mod, w_main, w_lora, w_gate, mu_p, l, tm):
    nb, s, d = x.shape
    hpb = tm // SHIFT_HALO
    layer_w = lambda width: pl.BlockSpec((1, d, width), lambda b, i: (l, 0, 0), pipeline_mode=pl.Buffered(1))
    return pl.pallas_call(
        functools.partial(_inproj_kernel, tm=tm),
        grid=(nb, s // tm),
        in_specs=[
            pl.BlockSpec((1, tm, d), lambda b, i: (b, i, 0)),
            pl.BlockSpec((1, SHIFT_HALO, d), lambda b, i: (b, jnp.maximum(i * hpb - 1, 0), 0)),
            pl.BlockSpec((1, 1, 6, d), lambda b, i: (l, b, 0, 0)),
            layer_w(D_POOL + 3 * D_RWKV), layer_w(D_LORA_PAD), layer_w(2 * D_MODEL),
            pl.BlockSpec((1, 1, D_SHIFT_PAD), lambda b, i: (l, 0, 0)),
        ],
        out_specs=[
            pl.BlockSpec((1, tm, D_POOL), lambda b, i: (b, i, 0)),
            pl.BlockSpec((1, tm, 3 * D_RWKV), lambda b, i: (b, i, 0)),
            pl.BlockSpec((1, tm, D_LORA_PAD), lambda b, i: (b, i, 0)),
            pl.BlockSpec((1, tm, 2 * D_MODEL), lambda b, i: (b, i, 0)),
        ],
        out_shape=[
            jax.ShapeDtypeStruct((nb, s, D_POOL), BF16),
            jax.ShapeDtypeStruct((nb, s, 3 * D_RWKV), BF16),
            jax.ShapeDtypeStruct((nb, s, D_LORA_PAD), BF16),
            jax.ShapeDtypeStruct((nb, s, 2 * D_MODEL), BF16),
        ],
        compiler_params=pltpu.CompilerParams(
            dimension_semantics=("parallel", "parallel"), vmem_limit_bytes=VMEM_LIMIT_BYTES),
        name="in_projection",
    )(x, x, mod, w_main, w_lora, w_gate, mu_p)


def _block_diag(y, bd_mask):
    return jnp.where(bd_mask, jnp.concatenate([y] * HEADS_PER_GROUP, axis=0), 0.0).astype(BF16)


def _mix_kernel(rkv_ref, lora_ref, wd_ref, wa_ref, wg_ref, vec_ref, o_ref, s_ref):
    c = pl.program_id(1)

    @pl.when(c == 0)
    def _():
        s_ref[...] = jnp.zeros(s_ref.shape, F32)

    row_g = lax.broadcasted_iota(jnp.int32, (GROUP, GROUP), 0)
    col_g = lax.broadcasted_iota(jnp.int32, (GROUP, GROUP), 1)
    bd_mask = (row_g // HEAD) == (col_g // HEAD)
    bd_ones = jnp.where(bd_mask, 1.0, 0.0).astype(BF16)
    row_c = lax.broadcasted_iota(jnp.int32, (CHUNK, GROUP), 0)
    col_c = lax.broadcasted_iota(jnp.int32, (CHUNK, GROUP), 1) % HEAD
    strict = col_c < row_c
    incl = col_c <= row_c
    eye_pk = jnp.where(col_c == row_c, 1.0, 0.0)
    tri2 = jnp.where(lax.broadcasted_iota(jnp.int32, (CHUNK, 2 * CHUNK), 1) % CHUNK
                     <= lax.broadcasted_iota(jnp.int32, (CHUNK, 2 * CHUNK), 0), 1.0, 0.0).astype(BF16)

    chains = [(b, g) for b in range(MIX_BATCH) for g in range(N_GROUPS)]
    n_ch = len(chains)
    G = range(n_ch)
    cols = [(g * GROUP, (g + 1) * GROUP) for _, g in chains]

    def vec(row, i):
        return vec_ref[0, row:row + 1, cols[i][0]:cols[i][1]]

    def rows(stacked, i):
        return stacked[i * CHUNK:(i + 1) * CHUNK]

    def seg_sum(parts):
        return _dot(jnp.concatenate([p.astype(BF16) for p in parts], axis=0), bd_ones)

    r = [rkv_ref[b, :, c0:c1].astype(F32) for (b, _), (c0, c1) in zip(chains, cols)]
    k = [rkv_ref[b, :, D_RWKV + c0:D_RWKV + c1].astype(F32) for (b, _), (c0, c1) in zip(chains, cols)]
    v = [rkv_ref[b, :, 2 * D_RWKV + c0:2 * D_RWKV + c1].astype(F32) for (b, _), (c0, c1) in zip(chains, cols)]

    tanh_xw = jnp.concatenate([lora_ref[b, :, LORA_W_OFF:LORA_A_OFF] for b in range(MIX_BATCH)], axis=0)
    xa = jnp.concatenate([lora_ref[b, :, LORA_A_OFF:LORA_G_OFF] for b in range(MIX_BATCH)], axis=0)
    sig_xg = jnp.concatenate([lora_ref[b, :, LORA_G_OFF:] for b in range(MIX_BATCH)], axis=0)
    z_g = [_dot(tanh_xw, wd_ref[0, :, g * GROUP:(g + 1) * GROUP]) for g in range(N_GROUPS)]
    a_g = [_dot(xa, wa_ref[0, :, g * GROUP:(g + 1) * GROUP]) for g in range(N_GROUPS)]
    gate_g = [_dot(sig_xg, wg_ref[0, :, g * GROUP:(g + 1) * GROUP]) for g in range(N_GROUPS)]
    ld = [LOG2_DECAY_SCALE * jax.nn.sigmoid(vec(0, i) + rows(z_g[g], b)) for i, (b, g) in enumerate(chains)]
    a = [jax.nn.sigmoid(vec(1, i) + rows(a_g[g], b)) for i, (b, g) in enumerate(chains)]
    gate = [rows(gate_g[g], b) for (b, g) in chains]

    kk = [k[i] * vec(2, i) for i in G]
    n2 = seg_sum([kk[i] * kk[i] for i in G])
    kkn = [kk[i] * jnp.minimum(lax.rsqrt(rows(n2, i)), 1e12) for i in G]
    k2 = [k[i] * (1.0 + (a[i] - 1.0) * vec(3, i)) for i in G]

    ld_hi = [ld[i].astype(BF16) for i in G]
    ld_lo = [(ld[i] - ld_hi[i].astype(F32)).astype(BF16) for i in G]
    cum = [_dot(tri2, jnp.concatenate([ld_hi[i], ld_lo[i]], axis=0)) for i in G]
    w_inc = [jnp.exp2(cum[i]) for i in G]
    w_exc = [jnp.exp2(cum[i] - ld[i]) for i in G]
    w_inv = [jnp.exp2(-cum[i]) for i in G]

    rt = [(r[i] * w_inc[i]).astype(BF16) for i in G]
    at = [(-kkn[i] * w_exc[i]).astype(BF16) for i in G]
    bt_f = [kkn[i] * a[i] * w_inv[i] for i in G]
    kt_f = [k2[i] * w_inv[i] for i in G]
    lhs_ar = [jnp.concatenate([at[i], rt[i]], axis=0) for i in G]

    a1 = [_dot_nt(lhs_ar[i], _block_diag(bt_f[i], bd_mask)) for i in G]
    a2 = [_dot_nt(lhs_ar[i], _block_diag(kt_f[i], bd_mask)) for i in G]
    aab = [jnp.where(strict, a1[i][:CHUNK], 0.0) for i in G]
    arb = [jnp.where(incl, a1[i][CHUNK:], 0.0) for i in G]
    aak = [jnp.where(strict, a2[i][:CHUNK], 0.0) for i in G]
    ark = [jnp.where(incl, a2[i][CHUNK:], 0.0) for i in G]

    s = [s_ref[b, g] for b, g in chains]
    xr = [_dot_nt(lhs_ar[i], jnp.where(bd_mask, s[i], 0.0).astype(BF16)) for i in G]
    av = [_dot(jnp.concatenate([aak[i], ark[i]], axis=0).astype(BF16), _block_diag(v[i], bd_mask)) for i in G]
    x = [xr[i][:CHUNK] + av[i][:CHUNK] for i in G]

    t = [eye_pk + aab[i] for i in G]
    p = [_dot(aab[i].astype(BF16), _block_diag(aab[i], bd_mask)) for i in G]
    for _ in range(4):
        pt = [_dot(jnp.concatenate([p[i], t[i]], axis=0).astype(BF16), _block_diag(p[i], bd_mask)) for i in G]
        t = [t[i] + pt[i][CHUNK:] for i in G]
        p = [pt[i][:CHUNK] for i in G]
    t = [t[i] + _dot(t[i].astype(BF16), _block_diag(p[i], bd_mask)) for i in G]

    u = [_dot(t[i].astype(BF16), _block_diag(x[i], bd_mask)) for i in G]
    o = [xr[i][CHUNK:] + av[i][CHUNK:] + _dot(arb[i].astype(BF16), _block_diag(u[i], bd_mask)) for i in G]
    m = [_dot_tn(jnp.concatenate([u[i].astype(BF16), v[i].astype(BF16)], axis=0),
                 jnp.concatenate([bt_f[i].astype(BF16), kt_f[i].astype(BF16)], axis=0)) for i in G]
    for i, (b, g) in enumerate(chains):
        s_ref[b, g] = (s[i] + m[i]) * w_inc[i][CHUNK - 1:CHUNK, :]

    mean = seg_sum(o) * (1.0 / HEAD)
    d = [o[i] - rows(mean, i) for i in G]
    var = seg_sum([d[i] * d[i] for i in G]) * (1.0 / HEAD)
    rk = seg_sum([r[i] * k2[i] * vec(4, i) for i in G])
    for i, (b, g) in enumerate(chains):
        on = d[i] * lax.rsqrt(rows(var, i) + GN_EPS) * vec(5, i) + vec(6, i)
        o_ref[b, :, cols[i][0]:cols[i][1]] = ((on + rows(rk, i) * v[i]) * gate[i]).astype(BF16)


def _mixer(rkv, lora, wd, wa, wg, vecs, l):
    nb, s, _ = rkv.shape
    layer = lambda b, c: (l, 0, 0)
    return pl.pallas_call(
        _mix_kernel,
        grid=(nb // MIX_BATCH, s // CHUNK),
        in_specs=[
            pl.BlockSpec((MIX_BATCH, CHUNK, 3 * D_RWKV), lambda b, c: (b, c, 0)),
            pl.BlockSpec((MIX_BATCH, CHUNK, D_LORA_PAD), lambda b, c: (b, c, 0)),
            pl.BlockSpec((1, 128, D_RWKV), layer),
            pl.BlockSpec((1, 128, D_RWKV), layer),
            pl.BlockSpec((1, 256, D_RWKV), layer),
            pl.BlockSpec((1, 8, D_RWKV), layer),
        ],
        out_specs=pl.BlockSpec((MIX_BATCH, CHUNK, D_RWKV), lambda b, c: (b, c, 0)),
        out_shape=jax.ShapeDtypeStruct((nb, s, D_RWKV), BF16),
        scratch_shapes=[pltpu.VMEM((MIX_BATCH, N_GROUPS, GROUP, GROUP), F32)],
        compiler_params=pltpu.CompilerParams(
            dimension_semantics=("arbitrary", "arbitrary"), vmem_limit_bytes=VMEM_LIMIT_BYTES),
        name="rwkv7_mixer",
    )(rkv, lora, wd, wa, wg, vecs)


def _out_kernel(pool_ref, halo_ref, o_ref, gate_ref, x_ref, mod_ref, pmix_ref, pscale_ref, wpu_ref,
                wru_ref, wout_ref, lng_ref, lnb_ref, xo_ref, *, alpha, tm):
    i = pl.program_id(1)
    halo = jnp.where(i == 0, 0.0, halo_ref[0].astype(F32))
    ext = jnp.concatenate([halo, pool_ref[0].astype(F32)], axis=0)
    t_idx = i * tm + lax.broadcasted_iota(jnp.int32, (tm, POOL_GC), 0)
    ms = []
    for gi, w in enumerate(POOL_WINDOWS):
        e = ext[:, gi * POOL_GC:(gi + 1) * POOL_GC]
        acc, span = e, 1
        while span < w:
            acc = acc + pltpu.roll(acc, span, 0)
            span *= 2
        cnt = jnp.minimum(t_idx + 1, w).astype(F32)
        ms.append((acc[POOL_HALO:] / cnt - e[POOL_HALO:]).astype(BF16))

    rs = [slice(r * (tm // OUT_SUB), (r + 1) * (tm // OUT_SUB)) for r in range(OUT_SUB)]
    n_pw = len(POOL_WINDOWS)
    mixed_pool = [[_dot(ms[gi][r], pmix_ref[0, gi]) * pscale_ref[0, :, gi * POOL_GC:(gi + 1) * POOL_GC]
                   for gi in range(n_pw)] for r in rs]
    m_all = [jnp.concatenate(mp, axis=1).astype(BF16) for mp in mixed_pool]
    y_pool = [_dot(m, wpu_ref[0]) for m in m_all]
    y_rwkv = [_dot(o_ref[0, r, :], wru_ref[0]) for r in rs]
    y = [(gate_ref[0, r, 0:D_MODEL].astype(F32) * yp + gate_ref[0, r, D_MODEL:].astype(F32) * yr).astype(BF16)
         for r, yp, yr in zip(rs, y_pool, y_rwkv)]
    mixed = [_dot(yy, wout_ref[0]) for yy in y]
    for r, mx in zip(rs, mixed):
        z = alpha * x_ref[0, r, :] + (1.0 + mod_ref[0, 0, 2:3, :]) * mx
        xo_ref[0, r, :] = _layer_norm(z, lng_ref[0], lnb_ref[0], LN_EPS)


def _out_stage(pool, o, gates, x, mod, pmix, pscale, wpu, wru, wout, lng, lnb, l, tm, alpha):
    nb, s, d = x.shape
    layer = lambda b, i: (l, 0, 0)
    hpb = tm // POOL_HALO
    return pl.pallas_call(
        functools.partial(_out_kernel, alpha=alpha, tm=tm),
        grid=(nb, s // tm),
        in_specs=[
            pl.BlockSpec((1, tm, D_POOL), lambda b, i: (b, i, 0)),
            pl.BlockSpec((1, POOL_HALO, D_POOL), lambda b, i: (b, jnp.maximum(i * hpb - 1, 0), 0)),
            pl.BlockSpec((1, tm, D_RWKV), lambda b, i: (b, i, 0)),
            pl.BlockSpec((1, tm, 2 * D_MODEL), lambda b, i: (b, i, 0)),
            pl.BlockSpec((1, tm, d), lambda b, i: (b, i, 0)),
            pl.BlockSpec((1, 1, 6, d), lambda b, i: (l, b, 0, 0)),
            pl.BlockSpec((1, len(POOL_WINDOWS), POOL_GC, POOL_GC), lambda b, i: (l, 0, 0, 0)),
            pl.BlockSpec((1, 1, D_POOL), layer),
            pl.BlockSpec((1, D_POOL, d), layer),
            pl.BlockSpec((1, D_RWKV, d), layer),
            pl.BlockSpec((1, d, d), layer),
            pl.BlockSpec((1, 1, d), layer),
            pl.BlockSpec((1, 1, d), layer),
        ],
        out_specs=pl.BlockSpec((1, tm, d), lambda b, i: (b, i, 0)),
        out_shape=jax.ShapeDtypeStruct((nb, s, d), F32),
        compiler_params=pltpu.CompilerParams(
            dimension_semantics=("parallel", "parallel"), vmem_limit_bytes=VMEM_LIMIT_BYTES),
        name="merge_out_projection",
    )(pool, pool, o, gates, x, mod, pmix, pscale, wpu, wru, wout, lng, lnb)


def _ffn_kernel(x_ref, mod_ref, w1_ref, w2_ref, lng_ref, lnb_ref, xo_ref, hid_ref, *, alpha):
    x = x_ref[0]
    h = (x * (1.0 + mod_ref[0, 0, 4:5, :]) + mod_ref[0, 0, 3:4, :]).astype(BF16)
    nt = IN_COL_TILE
    for j in range(0, D_FF, nt):
        a = jnp.maximum(_dot(h, w1_ref[0, :, j:j + nt]), 0.0)
        hid_ref[:, j:j + nt] = (a * a).astype(BF16)
    f = _dot(hid_ref[...], w2_ref[0])
    z = alpha * x + (1.0 + mod_ref[0, 0, 5:6, :]) * f
    xo_ref[0] = _layer_norm(z, lng_ref[0], lnb_ref[0], LN_EPS)


def _ffn_stage(x, mod, w1, w2, lng, lnb, l, tm, alpha):
    nb, s, d = x.shape
    layer = lambda b, i: (l, 0, 0)
    return pl.pallas_call(
        functools.partial(_ffn_kernel, alpha=alpha),
        grid=(nb, s // tm),
        in_specs=[
            pl.BlockSpec((1, tm, d), lambda b, i: (b, i, 0)),
            pl.BlockSpec((1, 1, 6, d), lambda b, i: (l, b, 0, 0)),
            pl.BlockSpec((1, d, D_FF), layer, pipeline_mode=pl.Buffered(1)),
            pl.BlockSpec((1, D_FF, d), layer, pipeline_mode=pl.Buffered(1)),
            pl.BlockSpec((1, 1, d), layer),
            pl.BlockSpec((1, 1, d), layer),
        ],
        out_specs=pl.BlockSpec((1, tm, d), lambda b, i: (b, i, 0)),
        out_shape=jax.ShapeDtypeStruct((nb, s, d), F32),
        scratch_shapes=[pltpu.VMEM((tm, D_FF), BF16)],
        compiler_params=pltpu.CompilerParams(
            dimension_semantics=("parallel", "parallel"), vmem_limit_bytes=VMEM_LIMIT_BYTES),
        name="relu2_mlp",
    )(x, mod, w1, w2, lng, lnb)


def _pad_last(w, width):
    return jnp.pad(w, [(0, 0)] * (w.ndim - 1) + [(0, width - w.shape[-1])])


def _pad_rows(w, height):
    return jnp.pad(w, ((0, 0), (0, height - w.shape[1]), (0, 0)))


WPREP_ROWS = 256


def _wprep_kernel(w_ref, main_ref, lora_ref, gate_ref):
    o_l = D_POOL + 3 * D_RWKV
    o_a = o_l + D_DECAY_LORA
    o_g = o_a + D_AAA_LORA
    o_gate = o_g + D_GATE_LORA
    main_ref[0] = w_ref[0, :, 0:o_l].astype(BF16)
    lora_ref[0] = jnp.zeros(lora_ref.shape[1:], BF16)
    lora_ref[0, :, LORA_W_OFF:LORA_W_OFF + D_DECAY_LORA] = w_ref[0, :, o_l:o_a].astype(BF16)
    lora_ref[0, :, LORA_A_OFF:LORA_A_OFF + D_AAA_LORA] = w_ref[0, :, o_a:o_g].astype(BF16)
    lora_ref[0, :, LORA_G_OFF:LORA_G_OFF + D_GATE_LORA] = w_ref[0, :, o_g:o_gate].astype(BF16)
    gate_ref[0] = w_ref[0, :, o_gate:].astype(BF16)


def _prep_in_weights(w_in, mu_shift):
    nl, d, d_in = w_in.shape
    o_l = D_POOL + 3 * D_RWKV
    blk = lambda width: pl.BlockSpec((1, WPREP_ROWS, width), lambda l, r: (l, r, 0))
    w_main, w_lora, w_gate = pl.pallas_call(
        _wprep_kernel,
        grid=(nl, d // WPREP_ROWS),
        in_specs=[blk(d_in)],
        out_specs=[blk(o_l), blk(D_LORA_PAD), blk(2 * D_MODEL)],
        out_shape=[jax.ShapeDtypeStruct((nl, d, o_l), BF16),
                   jax.ShapeDtypeStruct((nl, d, D_LORA_PAD), BF16),
                   jax.ShapeDtypeStruct((nl, d, 2 * D_MODEL), BF16)],
        compiler_params=pltpu.CompilerParams(
            dimension_semantics=("parallel", "parallel"), vmem_limit_bytes=VMEM_LIMIT_BYTES),
        name="in_weight_split",
    )(w_in)
    m = mu_shift[:, None, :]
    r3 = 3 * D_RWKV
    mu_p = jnp.concatenate([
        m[..., :r3],
        _pad_last(m[..., r3:r3 + D_DECAY_LORA], LORA_A_OFF - LORA_W_OFF),
        _pad_last(m[..., r3 + D_DECAY_LORA:r3 + D_DECAY_LORA + D_AAA_LORA], LORA_G_OFF - LORA_A_OFF),
        _pad_last(m[..., r3 + D_DECAY_LORA + D_AAA_LORA:], D_LORA_PAD - LORA_G_OFF),
    ], axis=-1)
    return w_main, w_lora, w_gate, mu_p


def kernel(x, c, w_ada, b_ada, w_in, mu_shift, pool_mix, pool_scale, w_pool_up, w0, w_decay_up, a0,
           w_aaa_up, w_gate_up, k_k, k_a, r_k, gn_g, gn_b, w_rwkv_up, w_out, ln1_g, ln1_b, w_ff1, w_ff2,
           ln2_g, ln2_b):
    nb, s, d = x.shape
    depth = w_ada.shape[0]
    assert d == D_MODEL and s % CHUNK == 0 and nb % MIX_BATCH == 0
    alpha = (2 * depth) ** 0.25
    tm = min(ROW_TILE, s)
    assert s % tm == 0 and tm % POOL_HALO == 0

    mod = _modulation(c, w_ada, b_ada).reshape(depth, nb, 6, d)
    w_main, w_lora, w_gate, mu_p = _prep_in_weights(w_in, mu_shift)
    vecs = jnp.stack([w0, a0, k_k, k_a, r_k.reshape(depth, D_RWKV), gn_g, gn_b, jnp.zeros_like(w0)], axis=1)
    wd = _pad_rows(w_decay_up, LORA_A_OFF - LORA_W_OFF).astype(BF16)
    wa = _pad_rows(w_aaa_up, LORA_G_OFF - LORA_A_OFF).astype(BF16)
    wg = _pad_rows(w_gate_up, D_LORA_PAD - LORA_G_OFF).astype(BF16)
    pmix, wpu, wru, wout = (w.astype(BF16) for w in (pool_mix, w_pool_up, w_rwkv_up, w_out))
    w1, w2 = w_ff1.astype(BF16), w_ff2.astype(BF16)
    pscale, g1, b1, g2, b2 = (v[:, None, :] for v in (pool_scale, ln1_g, ln1_b, ln2_g, ln2_b))

    for l in range(depth):
        pool, rkv, lora, gates = _inproj(x, mod, w_main, w_lora, w_gate, mu_p, l, tm)
        o = _mixer(rkv, lora, wd, wa, wg, vecs, l)
        x = _out_stage(pool, o, gates, x, mod, pmix, pscale, wpu, wru, wout, g1, b1, l, tm, alpha)
        x = _ffn_stage(x, mod, w1, w2, g2, b2, l, tm, alpha)
    return x
```
